```python
import jax, jax.numpy as jnp
from jax import lax
import numpy as np

D_MODEL = 1024
BATCH = 16
SEQ = 2048
DEPTH = 1

N_HEADS_A = 8
N_KV_HEADS_A = 2
HEAD_DIM_A = 64
GROUP_A = N_HEADS_A // N_KV_HEADS_A
N_IDX_HEADS = 4
IDX_DIM = 64
TOPK_MAX = 256
N_HEADS_B = 8
Q_LORA = 384
KV_LORA = 256
NOPE_DIM = 64
ROPE_DIM = 32
V_DIM_B = 64
ROPE_THETA = 10000.0
D_FF = 4 * D_MODEL
PLE_DIM = 256
Q_BLOCK = 128
EPS = 1e-6

IN_SIZES = (
    N_HEADS_A * HEAD_DIM_A,
    N_KV_HEADS_A * HEAD_DIM_A,
    N_KV_HEADS_A * HEAD_DIM_A,
    N_IDX_HEADS * IDX_DIM,
    IDX_DIM,
    N_IDX_HEADS,
    Q_LORA,
    KV_LORA,
    ROPE_DIM,
    D_MODEL,
    D_MODEL,
)
D_IN = sum(IN_SIZES)

kernel_name = "hybrid_dsa_mla_gated_block"


def _rmsnorm(x, g):
    xf = x.astype(jnp.float32)
    y = xf * lax.rsqrt(jnp.mean(xf * xf, axis=-1, keepdims=True) + EPS)
    return (y * g.astype(jnp.float32)).astype(x.dtype)


def _split_last(z, sizes):
    outs, start = [], 0
    for s in sizes:
        outs.append(z[..., start:start + s])
        start += s
    return outs


def _alibi_slopes(n):
    return 2.0 ** (-8.0 * jnp.arange(1, n + 1, dtype=jnp.float32) / n)


def _rope(x, positions):
    half = ROPE_DIM // 2
    inv = ROPE_THETA ** (-jnp.arange(half, dtype=jnp.float32) / half)
    ang = positions.astype(jnp.float32)[:, :, None, None] * inv
    cos, sin = jnp.cos(ang), jnp.sin(ang)
    xf = x.astype(jnp.float32)
    x1, x2 = xf[..., :half], xf[..., half:]
    out = jnp.concatenate([x1 * cos - x2 * sin, x2 * cos + x1 * sin], axis=-1)
    return out.astype(x.dtype)


def _to_blocks(a):
    b, s = a.shape[:2]
    a = a.reshape((b, s // Q_BLOCK, Q_BLOCK) + a.shape[2:])
    return jnp.moveaxis(a, 1, 0)


def _from_blocks(o):
    nb, b, qb = o.shape[:3]
    return jnp.moveaxis(o, 0, 1).reshape((b, nb * qb) + o.shape[3:])


def _dsa_attention(q, k, v, q_idx, k_idx, w_idx, positions, topk):
    b, s = q.shape[:2]
    s_idx = jnp.arange(s)
    slopes = _alibi_slopes(N_HEADS_A).reshape(N_KV_HEADS_A, GROUP_A)
    scale = HEAD_DIM_A ** -0.5

    def block(args):
        qb, qib, wb, pb, tb = args
        rel = jax.nn.relu(jnp.einsum('bqhd,bsd->bqhs', qib, k_idx))
        score = jnp.einsum('bqhs,bqh->bqs', rel, wb).astype(jnp.float32)
        causal = s_idx[None, None, :] <= tb[None, :, None]
        score = jnp.where(causal, score, -jnp.inf)
        _, sel = lax.top_k(score, topk)
        k_sel = jax.vmap(lambda kr, ir: kr[ir])(k, sel)
        v_sel = jax.vmap(lambda vr, ir: vr[ir])(v, sel)
        p_sel = jax.vmap(lambda pr, ir: pr[ir])(positions, sel)
        valid = sel <= tb[None, :, None]
        qg = qb.reshape(qb.shape[0], qb.shape[1], N_KV_HEADS_A, GROUP_A, HEAD_DIM_A)
        logits = jnp.einsum('bqcgd,bqncd->bqcgn', qg, k_sel).astype(jnp.float32) * scale
        dist = jnp.abs(pb[:, :, None] - p_sel).astype(jnp.float32)
        logits = logits - slopes[None, None, :, :, None] * dist[:, :, None, None, :]
        logits = jnp.where(valid[:, :, None, None, :], logits, -jnp.inf)
        probs = jax.nn.softmax(logits, axis=-1).astype(v.dtype)
        o = jnp.einsum('bqcgn,bqncd->bqcgd', probs, v_sel)
        return o.reshape(o.shape[0], o.shape[1], N_HEADS_A * HEAD_DIM_A)

    t_blocks = jnp.arange(s).reshape(s // Q_BLOCK, Q_BLOCK)
    out = lax.map(block, (_to_blocks(q), _to_blocks(q_idx), _to_blocks(w_idx),
                          _to_blocks(positions), t_blocks))
    return _from_blocks(out)


def _mla_attention(q, k, v):
    s = q.shape[1]
    s_idx = jnp.arange(s)
    scale = (NOPE_DIM + ROPE_DIM) ** -0.5

    def block(args):
        qb, tb = args
        logits = jnp.einsum('bqhd,bshd->bhqs', qb, k).astype(jnp.float32) * scale
        causal = s_idx[None, None, None, :] <= tb[None, None, :, None]
        logits = jnp.where(causal, logits, -jnp.inf)
        probs = jax.nn.softmax(logits, axis=-1).astype(v.dtype)
        o = jnp.einsum('bhqs,bshd->bqhd', probs, v)
        return o.reshape(o.shape[0], o.shape[1], N_HEADS_B * V_DIM_B)

    t_blocks = jnp.arange(s).reshape(s // Q_BLOCK, Q_BLOCK)
    out = lax.map(block, (_to_blocks(q), t_blocks))
    return _from_blocks(out)


def _dense(key, shape, fan_in):
    return jax.random.normal(key, shape, jnp.float32) * fan_in ** -0.5


def _gain(key, shape):
    return 1.0 + 0.02 * jax.random.normal(key, shape, jnp.float32)


def setup_inputs(seed: int = 0) -> dict:
    key = jax.random.key(seed)
    ks = jax.random.split(key, 20)
    x = jax.random.normal(ks[0], (BATCH, SEQ, D_MODEL), jnp.float32)
    p = jax.random.normal(ks[1], (DEPTH, BATCH, SEQ, PLE_DIM), jnp.float32)
    offsets = jax.random.randint(ks[2], (BATCH, 1), 0, 4096, dtype=jnp.int32)
    positions = (jnp.arange(SEQ, dtype=jnp.int32)[None, :] + offsets).astype(jnp.int32)
    L = DEPTH
    return {
        "x": x,
        "p": p,
        "positions": positions,
        "g_mix": _gain(ks[3], (L, D_MODEL)),
        "w_in": _dense(ks[4], (L, D_MODEL, D_IN), D_MODEL),
        "g_cq": _gain(ks[5], (L, Q_LORA)),
        "w_uq": _dense(ks[6], (L, Q_LORA, N_HEADS_B * (NOPE_DIM + ROPE_DIM)), Q_LORA),
        "g_ckv": _gain(ks[7], (L, KV_LORA)),
        "w_ukv": _dense(ks[8], (L, KV_LORA, N_HEADS_B * (NOPE_DIM + V_DIM_B)), KV_LORA),
        "w_pa": _dense(ks[9], (L, N_HEADS_A * HEAD_DIM_A, D_MODEL), N_HEADS_A * HEAD_DIM_A),
        "w_pb": _dense(ks[10], (L, N_HEADS_B * V_DIM_B, D_MODEL), N_HEADS_B * V_DIM_B),
        "w_out": _dense(ks[11], (L, D_MODEL, D_MODEL), D_MODEL),
        "g_mlp": _gain(ks[12], (L, D_MODEL)),
        "w_ff1": _dense(ks[13], (L, D_MODEL, D_FF), D_MODEL),
        "w_ff2": _dense(ks[14], (L, D_FF, D_MODEL), D_FF),
        "g_ple": _gain(ks[15], (L, D_MODEL)),
        "w_ple_gate": _dense(ks[16], (L, D_MODEL, D_MODEL), D_MODEL),
        "w_ple": _dense(ks[17], (L, PLE_DIM, D_MODEL), PLE_DIM),
        "g_final": _gain(ks[18], (D_MODEL,)),
    }


def reference(x, p, positions, g_mix, w_in, g_cq, w_uq, g_ckv, w_ukv, w_pa, w_pb,
              w_out, g_mlp, w_ff1, w_ff2, g_ple, w_ple_gate, w_ple, g_final):
    b, s, _ = x.shape
    topk = min(TOPK_MAX, s // 4)
    idx_scale = (N_IDX_HEADS * IDX_DIM) ** -0.5
    h = x
    for i in range(DEPTH):
        n = _rmsnorm(h, g_mix[i])
        z = n @ w_in[i]
        qa, ka, va, qi, ki, wi, cq, ckv, kr, ga, gb = _split_last(z, IN_SIZES)
        y_a = _dsa_attention(
            qa.reshape(b, s, N_HEADS_A, HEAD_DIM_A),
            ka.reshape(b, s, N_KV_HEADS_A, HEAD_DIM_A),
            va.reshape(b, s, N_KV_HEADS_A, HEAD_DIM_A),
            qi.reshape(b, s, N_IDX_HEADS, IDX_DIM),
            ki,
            wi * idx_scale,
            positions, topk)
        q_b = (_rmsnorm(cq, g_cq[i]) @ w_uq[i]).reshape(b, s, N_HEADS_B, NOPE_DIM + ROPE_DIM)
        q_b = jnp.concatenate([q_b[..., :NOPE_DIM], _rope(q_b[..., NOPE_DIM:], positions)], axis=-1)
        kv = (_rmsnorm(ckv, g_ckv[i]) @ w_ukv[i]).reshape(b, s, N_HEADS_B, NOPE_DIM + V_DIM_B)
        k_rope = jnp.broadcast_to(_rope(kr[:, :, None, :], positions), (b, s, N_HEADS_B, ROPE_DIM))
        k_b = jnp.concatenate([kv[..., :NOPE_DIM], k_rope], axis=-1)
        v_b = kv[..., NOPE_DIM:]
        y_b = _mla_attention(q_b, k_b, v_b)
        merged = jax.nn.sigmoid(ga) * (y_a @ w_pa[i]) + jax.nn.sigmoid(gb) * (y_b @ w_pb[i])
        h = h + merged @ w_out[i]
        n2 = _rmsnorm(h, g_mlp[i])
        h = h + jnp.square(jax.nn.relu(n2 @ w_ff1[i])) @ w_ff2[i]
        n3 = _rmsnorm(h, g_ple[i])
        h = h + jax.nn.sigmoid(n3 @ w_ple_gate[i]) * (p[i] @ w_ple[i])
    return _rmsnorm(h, g_final)
```

```python
import functools

import jax
import jax.numpy as jnp
import numpy as np
from jax import lax
from jax.experimental import pallas as pl
from jax.experimental.pallas import tpu as pltpu

D_MODEL = 1024
N_HEADS_A = 8
N_KV_HEADS_A = 2
HEAD_DIM_A = 64
GROUP_A = N_HEADS_A // N_KV_HEADS_A
N_IDX_HEADS = 4
IDX_DIM = 64
TOPK_MAX = 256
N_HEADS_B = 8
Q_LORA = 384
KV_LORA = 256
NOPE_DIM = 64
ROPE_DIM = 32
HALF_ROPE = ROPE_DIM // 2
V_DIM_B = 64
ROPE_THETA = 10000.0
D_FF = 4 * D_MODEL
PLE_DIM = 256
EPS = 1e-6

LANE = 128
SUBLANE = 8
HEAD_SLOT = 128
VMEM_LIMIT = 56 * 1024 * 1024

TM_PROJ = 512
TM_POST = 512
TQ = 256
KC = 256
FF_CHUNK = 1024
NEG = -1e30

C_QA = 0
C_QI = C_QA + N_HEADS_A * HEAD_DIM_A
C_KA = C_QI + N_IDX_HEADS * IDX_DIM
C_KI = C_KA + 2 * N_KV_HEADS_A * LANE
C_CQ = C_KI + 2 * LANE
C_CKV = C_CQ + Q_LORA
C_KR = C_CKV + KV_LORA
C_GA = C_KR + LANE
C_GB = C_GA + D_MODEL
C_END = C_GB + D_MODEL

_NT = (((1,), (1,)), ((), ()))
_TN = (((0,), (0,)), ((), ()))


def _rms(x, g):
    return x * lax.rsqrt(jnp.mean(x * x, axis=-1, keepdims=True) + EPS) * g


def _rope_slot(x, cos_t, sin_dn, sin_up):
    up = pltpu.roll(x, HALF_ROPE, 1)
    dn = pltpu.roll(x, LANE - HALF_ROPE, 1)
    return x * cos_t + dn * sin_dn + up * sin_up


def _proj_kernel(x_ref, pos_ref, invl_ref, gmix_ref, win_ref, wvat_ref, wwit_ref, gcq_ref,
                 wuq_ref, gckv_ref, wukvk_ref, wukvvt_ref,
                 qa_ref, qi_ref, qb_ref, wit_ref, kap_ref, kip_ref, kb_ref, vat_ref, vbt_ref,
                 sga_ref, sgb_ref):
    bf = jnp.bfloat16
    n = _rms(x_ref[...], gmix_ref[...]).astype(bf)

    def cols(c0, c1):
        return jnp.dot(n, win_ref[:, c0:c1], preferred_element_type=jnp.float32)

    qa_ref[...] = (cols(C_QA, C_QI) * (HEAD_DIM_A ** -0.5)).astype(bf)
    qi_ref[...] = cols(C_QI, C_KA).astype(bf)
    kap_ref[...] = cols(C_KA, C_KI).astype(bf)
    kip_ref[...] = cols(C_KI, C_CQ).astype(bf)
    sga_ref[...] = jax.nn.sigmoid(cols(C_GA, C_GB)).astype(bf)
    sgb_ref[...] = jax.nn.sigmoid(cols(C_GB, C_END)).astype(bf)
    def put_chunks(ref, vt):
        for j in range(TM_PROJ // KC):
            ref[j] = vt[:, j * KC:(j + 1) * KC].astype(bf)

    put_chunks(vat_ref, lax.dot_general(wvat_ref[...], n, _NT, preferred_element_type=jnp.float32))
    wit_ref[...] = lax.dot_general(wwit_ref[...], n, _NT,
                                   preferred_element_type=jnp.float32) * ((N_IDX_HEADS * IDX_DIM) ** -0.5)

    lane = lax.broadcasted_iota(jnp.int32, (1, LANE), 1)
    is_x1 = (lane >= NOPE_DIM) & (lane < NOPE_DIM + HALF_ROPE)
    is_x2 = (lane >= NOPE_DIM + HALF_ROPE) & (lane < NOPE_DIM + ROPE_DIM)
    ang = pos_ref[...] * invl_ref[...]
    cos, sin = jnp.cos(ang), jnp.sin(ang)
    cos_t = jnp.where(is_x1 | is_x2, cos, 1.0)
    sin_dn = jnp.where(is_x1, -sin, 0.0)
    sin_up = jnp.where(is_x2, sin, 0.0)

    cqn = _rms(cols(C_CQ, C_CKV), gcq_ref[...]).astype(bf)
    qscale = (NOPE_DIM + ROPE_DIM) ** -0.5
    for h in range(N_HEADS_B):
        sl = slice(h * HEAD_SLOT, (h + 1) * HEAD_SLOT)
        qh = jnp.dot(cqn, wuq_ref[:, sl], preferred_element_type=jnp.float32)
        qb_ref[:, sl] = (_rope_slot(qh, cos_t, sin_dn, sin_up) * qscale).astype(bf)

    ckvn = _rms(cols(C_CKV, C_KR), gckv_ref[...]).astype(bf)
    kr = _rope_slot(cols(C_KR, C_GA), cos_t, sin_dn, sin_up)
    for h in range(N_HEADS_B):
        sl = slice(h * HEAD_SLOT, (h + 1) * HEAD_SLOT)
        kh = jnp.dot(ckvn, wukvk_ref[:, sl], preferred_element_type=jnp.float32)
        kb_ref[:, sl] = (kh + kr).astype(bf)
    put_chunks(vbt_ref, lax.dot_general(wukvvt_ref[...], ckvn, _NT,
                                        preferred_element_type=jnp.float32))


def _const_spec(shape):
    return pl.BlockSpec(shape, lambda *_: (0,) * len(shape))


def _proj_call(x2, posc, invl, gmix, win, wvat, wwit, gcq, wuq, gckv, wukvk, wukvvt):
    n_tok = x2.shape[0]
    tm = TM_PROJ
    bf = jnp.bfloat16
    row = lambda w: pl.BlockSpec((tm, w), lambda t: (t, 0))
    col = lambda r: pl.BlockSpec((r, tm), lambda t: (0, t))
    chunks = lambda r: pl.BlockSpec((tm // KC, r, KC), lambda t: (t, 0, 0))
    in_specs = [row(D_MODEL), row(1), _const_spec(invl.shape), _const_spec(gmix.shape),
                _const_spec(win.shape), _const_spec(wvat.shape), _const_spec(wwit.shape),
                _const_spec(gcq.shape), _const_spec(wuq.shape), _const_spec(gckv.shape),
                _const_spec(wukvk.shape), _const_spec(wukvvt.shape)]
    out_shape = [
        jax.ShapeDtypeStruct((n_tok, C_QI - C_QA), bf),
        jax.ShapeDtypeStruct((n_tok, C_KA - C_QI), bf),
        jax.ShapeDtypeStruct((n_tok, N_HEADS_B * HEAD_SLOT), bf),
        jax.ShapeDtypeStruct((SUBLANE, n_tok), jnp.float32),
        jax.ShapeDtypeStruct((n_tok, C_KI - C_KA), bf),
        jax.ShapeDtypeStruct((n_tok, C_CQ - C_KI), bf),
        jax.ShapeDtypeStruct((n_tok, N_HEADS_B * HEAD_SLOT), bf),
        jax.ShapeDtypeStruct((n_tok // KC, N_KV_HEADS_A * HEAD_DIM_A, KC), bf),
        jax.ShapeDtypeStruct((n_tok // KC, N_HEADS_B * V_DIM_B, KC), bf),
        jax.ShapeDtypeStruct((n_tok, D_MODEL), bf),
        jax.ShapeDtypeStruct((n_tok, D_MODEL), bf),
    ]
    out_specs = [row(C_QI - C_QA), row(C_KA - C_QI), row(N_HEADS_B * HEAD_SLOT), col(SUBLANE),
                 row(C_KI - C_KA), row(C_CQ - C_KI), row(N_HEADS_B * HEAD_SLOT),
                 chunks(N_KV_HEADS_A * HEAD_DIM_A), chunks(N_HEADS_B * V_DIM_B),
                 row(D_MODEL), row(D_MODEL)]
    return pl.pallas_call(
        _proj_kernel,
        grid=(n_tok // tm,),
        in_specs=in_specs,
        out_specs=out_specs,
        out_shape=out_shape,
        compiler_params=pltpu.CompilerParams(
            dimension_semantics=("arbitrary",), vmem_limit_bytes=VMEM_LIMIT),
        name="proj",
    )(x2, posc, invl, gmix, win, wvat, wwit, gcq, wuq, gckv, wukvk, wukvvt)


def _col_sum(a):
    r = a.shape[0]
    part = jnp.sum(a.reshape(r // SUBLANE, SUBLANE, a.shape[1]), axis=0)
    return jnp.sum(part, axis=0, keepdims=True)


def _col_max(a):
    r = a.shape[0]
    part = jnp.max(a.reshape(r // SUBLANE, SUBLANE, a.shape[1]), axis=0)
    return jnp.max(part, axis=0, keepdims=True)


def _attn_kernel(qa_ref, qi_ref, qb_ref, wit_ref, posq_ref,
                 kap_ref, kip_ref, kb_ref, vat_ref, vbt_ref, posk_ref,
                 yat_ref, ybt_ref,
                 score_ref, acc_ref, m_ref, l_ref, *, topk):
    bf = jnp.bfloat16
    f32 = jnp.float32
    i = pl.program_id(1)
    nch = i + 1
    row_i = lax.broadcasted_iota(jnp.int32, (KC, TQ), 0)
    col_i = lax.broadcasted_iota(jnp.int32, (KC, TQ), 1)
    causal = row_i <= col_i
    kf = float(topk)

    def kslice(c):
        return pl.ds(pl.multiple_of(c * KC, KC), KC)

    w = wit_ref[...]

    def score_chunk(c, carry):
        acc = jnp.zeros((KC, TQ), f32)
        for h in range(N_IDX_HEADS):
            kk = kip_ref[kslice(c), (h % 2) * LANE:(h % 2 + 1) * LANE]
            qq = qi_ref[:, (h // 2) * LANE:(h // 2 + 1) * LANE]
            d = lax.dot_general(kk, qq, _NT, preferred_element_type=f32)
            acc = acc + jnp.maximum(d, 0.0) * w[h:h + 1, :]
        score_ref[c] = acc
        return carry

    lax.fori_loop(0, nch, score_chunk, 0)
    score_ref[i] = jnp.where(causal, score_ref[i], -jnp.inf)

    def count(pred):
        def body(c, acc):
            mk = jnp.where(pred(score_ref[c]), 1.0, 0.0)
            return acc + jnp.sum(mk.reshape(KC // SUBLANE, SUBLANE, TQ), axis=0)
        acc = lax.fori_loop(0, nch, body, jnp.zeros((SUBLANE, TQ), f32))
        return jnp.sum(acc, axis=0, keepdims=True)

    def key_to_float(key):
        bits = jnp.where(key >= 0, key, key ^ jnp.int32(0x7FFFFFFF))
        return lax.bitcast_convert_type(bits, f32)

    def search_step(j, key):
        cand = key ^ lax.shift_left(jnp.int32(1), jnp.int32(31) - j)
        thr = key_to_float(cand)
        cnt = count(lambda s: s >= thr)
        return jnp.where(cnt >= kf, cand, key)

    key0 = jnp.full((1, TQ), jnp.iinfo(jnp.int32).min, jnp.int32)
    key = lax.fori_loop(0, 32, search_step, key0)
    t_glob = i * TQ + lax.broadcasted_iota(jnp.int32, (1, TQ), 1)
    take_all = t_glob < topk
    thr = jnp.where(take_all, -jnp.inf, key_to_float(key))
    n_gt = count(lambda s: s > thr)
    need = jnp.where(take_all, 0.0, kf - n_gt)

    tri = jnp.where(lax.broadcasted_iota(jnp.int32, (KC, KC), 0)
                    >= lax.broadcasted_iota(jnp.int32, (KC, KC), 1), 1.0, 0.0).astype(bf)
    posq = posq_ref[...]

    def init_state():
        m_ref[...] = jnp.full(m_ref.shape, NEG, f32)
        l_ref[...] = jnp.zeros(l_ref.shape, f32)
        acc_ref[...] = jnp.zeros(acc_ref.shape, f32)

    def softmax_step(h, logits, vt):
        m_old = m_ref[h:h + 1, :]
        m_new = jnp.maximum(m_old, _col_max(logits))
        p = jnp.exp(logits - m_new)
        alpha = jnp.exp(m_old - m_new)
        l_ref[h:h + 1, :] = alpha * l_ref[h:h + 1, :] + _col_sum(p)
        pv = jnp.dot(vt, p.astype(bf), preferred_element_type=f32)
        acc_ref[h] = alpha * acc_ref[h] + pv
        m_ref[h:h + 1, :] = m_new

    def write_out(out_ref, dv):
        for h in range(N_HEADS_A):
            inv_l = 1.0 / l_ref[h:h + 1, :]
            out_ref[h * dv:(h + 1) * dv, :] = (acc_ref[h] * inv_l).astype(out_ref.dtype)

    init_state()

    def dsa_chunk(c, taken):
        s = score_ref[c]
        eq = s == thr
        pc = jnp.dot(tri, jnp.where(eq, 1.0, 0.0).astype(bf),
                     preferred_element_type=f32) + taken
        sel = (s > thr) | (eq & (pc <= need))
        dist = jnp.abs(posk_ref[kslice(c), :] - posq)
        for h in range(N_HEADS_A):
            g = h // GROUP_A
            kk = kap_ref[kslice(c), (2 * g + h % 2) * LANE:(2 * g + h % 2 + 1) * LANE]
            qq = qa_ref[:, (h // 2) * LANE:(h // 2 + 1) * LANE]
            slope = 2.0 ** (-8.0 * (h + 1) / N_HEADS_A)
            logits = lax.dot_general(kk, qq, _NT, preferred_element_type=f32) - slope * dist
            logits = jnp.where(sel, logits, NEG)
            vt = vat_ref[c, g * HEAD_DIM_A:(g + 1) * HEAD_DIM_A, :]
            softmax_step(h, logits, vt)
        return pc[KC - 1:KC, :]

    lax.fori_loop(0, nch, dsa_chunk, jnp.zeros((1, TQ), f32))
    write_out(yat_ref, HEAD_DIM_A)

    init_state()

    def mla_chunk(c, masked):
        for h in range(N_HEADS_B):
            kk = kb_ref[kslice(c), h * HEAD_SLOT:(h + 1) * HEAD_SLOT]
            qq = qb_ref[:, h * HEAD_SLOT:(h + 1) * HEAD_SLOT]
            logits = lax.dot_general(kk, qq, _NT, preferred_element_type=f32)
            if masked:
                logits = jnp.where(causal, logits, NEG)
            vt = vbt_ref[c, h * V_DIM_B:(h + 1) * V_DIM_B, :]
            softmax_step(h, logits, vt)

    def mla_body(c, carry):
        mla_chunk(c, False)
        return carry

    lax.fori_loop(0, i, mla_body, 0)
    mla_chunk(i, True)
    write_out(ybt_ref, V_DIM_B)


def _attn_call(qa, qi, qb, wit, posr, kap, kip, kb, vat, vbt, posc, batch, seq, topk):
    nq = seq // TQ
    bf = jnp.bfloat16
    n_tok = batch * seq
    qrow = lambda w: pl.BlockSpec((TQ, w), lambda b, i: (b * nq + i, 0))
    qcol = lambda r: pl.BlockSpec((r, TQ), lambda b, i: (0, b * nq + i))
    krow = lambda w: pl.BlockSpec((seq, w), lambda b, i: (b, 0))
    kchunks = lambda r: pl.BlockSpec((seq // KC, r, KC), lambda b, i: (b, 0, 0))
    in_specs = [qrow(qa.shape[1]), qrow(qi.shape[1]), qrow(qb.shape[1]), qcol(SUBLANE), qcol(1),
                krow(kap.shape[1]), krow(kip.shape[1]), krow(kb.shape[1]),
                kchunks(vat.shape[1]), kchunks(vbt.shape[1]), krow(1)]
    out_shape = [jax.ShapeDtypeStruct((N_HEADS_A * HEAD_DIM_A, n_tok), bf),
                 jax.ShapeDtypeStruct((N_HEADS_B * V_DIM_B, n_tok), bf)]
    out_specs = [qcol(N_HEADS_A * HEAD_DIM_A), qcol(N_HEADS_B * V_DIM_B)]
    scratch = [pltpu.VMEM((seq // KC, KC, TQ), jnp.float32),
               pltpu.VMEM((N_HEADS_A, HEAD_DIM_A, TQ), jnp.float32),
               pltpu.VMEM((N_HEADS_A, TQ), jnp.float32),
               pltpu.VMEM((N_HEADS_A, TQ), jnp.float32)]
    return pl.pallas_call(
        functools.partial(_attn_kernel, topk=topk),
        grid=(batch, nq),
        in_specs=in_specs,
        out_specs=out_specs,
        out_shape=out_shape,
        scratch_shapes=scratch,
        compiler_params=pltpu.CompilerParams(
            dimension_semantics=("arbitrary", "arbitrary"), vmem_limit_bytes=VMEM_LIMIT),
        name="attn",
    )(qa, qi, qb, wit, posr, kap, kip, kb, vat, vbt, posc)


def _post_kernel(x_ref, yat_ref, ybt_ref, sga_ref, sgb_ref, p_ref,
                 wpa_ref, wpb_ref, wout_ref, gmlp_ref, wff1_ref, wff2_ref, gple_ref,
                 wpg_ref, wple_ref, gfin_ref, o_ref, *, final_norm):
    bf = jnp.bfloat16
    f32 = jnp.float32
    a = lax.dot_general(yat_ref[...], wpa_ref[...], _TN, preferred_element_type=f32)
    b = lax.dot_general(ybt_ref[...], wpb_ref[...], _TN, preferred_element_type=f32)
    merged = sga_ref[...].astype(f32) * a + sgb_ref[...].astype(f32) * b
    h = x_ref[...] + jnp.dot(merged.astype(bf), wout_ref[...], preferred_element_type=f32)
    n2 = _rms(h, gmlp_ref[...]).astype(bf)
    ff = jnp.zeros_like(h)
    for c in range(D_FF // FF_CHUNK):
        sl = slice(c * FF_CHUNK, (c + 1) * FF_CHUNK)
        u = jnp.maximum(jnp.dot(n2, wff1_ref[:, sl], preferred_element_type=f32), 0.0)
        ff = ff + jnp.dot((u * u).astype(bf), wff2_ref[sl, :], preferred_element_type=f32)
    h = h + ff
    n3 = _rms(h, gple_ref[...]).astype(bf)
    gate = jax.nn.sigmoid(jnp.dot(n3, wpg_ref[...], preferred_element_type=f32))
    pe = jnp.dot(p_ref[...].astype(bf), wple_ref[...], preferred_element_type=f32)
    h = h + gate * pe
    o_ref[...] = _rms(h, gfin_ref[...]) if final_norm else h


def _post_call(x2, yat, ybt, sga, sgb, p2, wpa, wpb, wout, gmlp, wff1, wff2, gple, wpg, wple, gfin,
               final_norm):
    n_tok = x2.shape[0]
    tm = TM_POST
    row = lambda w: pl.BlockSpec((tm, w), lambda t: (t, 0))
    col = lambda r: pl.BlockSpec((r, tm), lambda t: (0, t))
    wspec = lambda a: pl.BlockSpec(a.shape, lambda t: (0,) * a.ndim, pipeline_mode=pl.Buffered(1))
    in_specs = [row(D_MODEL), col(yat.shape[0]), col(ybt.shape[0]), row(D_MODEL), row(D_MODEL),
                row(PLE_DIM)] + [wspec(a) for a in
                                 (wpa, wpb, wout, gmlp, wff1, wff2, gple, wpg, wple, gfin)]
    return pl.pallas_call(
        functools.partial(_post_kernel, final_norm=final_norm),
        grid=(n_tok // tm,),
        in_specs=in_specs,
        out_specs=row(D_MODEL),
        out_shape=jax.ShapeDtypeStruct((n_tok, D_MODEL), jnp.float32),
        compiler_params=pltpu.CompilerParams(
            dimension_semantics=("arbitrary",), vmem_limit_bytes=VMEM_LIMIT),
        name="post",
    )(x2, yat, ybt, sga, sgb, p2, wpa, wpb, wout, gmlp, wff1, wff2, gple, wpg, wple, gfin)


def _pack_layer(w_in, w_uq, w_ukv):
    bf = jnp.bfloat16
    sizes = (N_HEADS_A * HEAD_DIM_A, N_KV_HEADS_A * HEAD_DIM_A, N_KV_HEADS_A * HEAD_DIM_A,
             N_IDX_HEADS * IDX_DIM, IDX_DIM, N_IDX_HEADS, Q_LORA, KV_LORA, ROPE_DIM,
             D_MODEL, D_MODEL)
    offs = np.concatenate([[0], np.cumsum(sizes)])
    wqa, wka, wva, wqi, wki, wwi, wcq, wckv, wkr, wga, wgb = [
        w_in[:, offs[j]:offs[j + 1]] for j in range(len(sizes))]
    d = w_in.shape[0]
    z64 = jnp.zeros((d, HEAD_DIM_A), w_in.dtype)
    ka_cols = []
    for g in range(N_KV_HEADS_A):
        kg = wka[:, g * HEAD_DIM_A:(g + 1) * HEAD_DIM_A]
        ka_cols += [kg, z64, z64, kg]
    ki_cols = [wki, z64, z64, wki]
    kr_cols = [jnp.zeros((d, NOPE_DIM), w_in.dtype), wkr,
               jnp.zeros((d, HEAD_SLOT - NOPE_DIM - ROPE_DIM), w_in.dtype)]
    win = jnp.concatenate([wqa, wqi] + ka_cols + ki_cols + [wcq, wckv] + kr_cols + [wga, wgb],
                          axis=1).astype(bf)
    wvat = wva.T.astype(bf)
    wwit = jnp.concatenate([wwi.T, jnp.zeros((SUBLANE - N_IDX_HEADS, d), w_in.dtype)],
                           axis=0).astype(bf)
    uq = w_uq.reshape(Q_LORA, N_HEADS_B, NOPE_DIM + ROPE_DIM)
    uq = jnp.pad(uq, ((0, 0), (0, 0), (0, HEAD_SLOT - NOPE_DIM - ROPE_DIM)))
    wuq = uq.reshape(Q_LORA, N_HEADS_B * HEAD_SLOT).astype(bf)
    ukv = w_ukv.reshape(KV_LORA, N_HEADS_B, NOPE_DIM + V_DIM_B)
    uk = jnp.pad(ukv[:, :, :NOPE_DIM], ((0, 0), (0, 0), (0, HEAD_SLOT - NOPE_DIM)))
    wukvk = uk.reshape(KV_LORA, N_HEADS_B * HEAD_SLOT).astype(bf)
    wukvvt = ukv[:, :, NOPE_DIM:].reshape(KV_LORA, N_HEADS_B * V_DIM_B).T.astype(bf)
    return win, wvat, wwit, wuq, wukvk, wukvvt


def kernel(x, p, positions, g_mix, w_in, g_cq, w_uq, g_ckv, w_ukv, w_pa, w_pb, w_out, g_mlp,
           w_ff1, w_ff2, g_ple, w_ple_gate, w_ple, g_final):
    bf = jnp.bfloat16
    batch, seq, d = x.shape
    depth = w_in.shape[0]
    n_tok = batch * seq
    topk = min(TOPK_MAX, seq // 4)
    assert d == D_MODEL and seq % TQ == 0 and TQ == KC and topk <= TQ and n_tok % TM_PROJ == 0

    posf = positions.astype(jnp.float32)
    posc = posf.reshape(n_tok, 1)
    posr = posf.reshape(1, n_tok)
    inv = ROPE_THETA ** (-jnp.arange(HALF_ROPE, dtype=jnp.float32) / HALF_ROPE)
    invl = jnp.concatenate([jnp.zeros((NOPE_DIM,), jnp.float32), inv, inv,
                            jnp.zeros((HEAD_SLOT - NOPE_DIM - ROPE_DIM,), jnp.float32)]).reshape(1, LANE)

    h = x.reshape(n_tok, d)
    for li in range(depth):
        win, wvat, wwit, wuq, wukvk, wukvvt = _pack_layer(w_in[li], w_uq[li], w_ukv[li])
        (qa, qi, qb, wit, kap, kip, kb, vat, vbt, sga, sgb) = _proj_call(
            h, posc, invl, g_mix[li].reshape(1, d), win, wvat, wwit, g_cq[li].reshape(1, -1),
            wuq, g_ckv[li].reshape(1, -1), wukvk, wukvvt)
        yat, ybt = _attn_call(qa, qi, qb, wit, posr, kap, kip, kb, vat, vbt, posc, batch, seq, topk)
        h = _post_call(h, yat, ybt, sga, sgb, p[li].reshape(n_tok, -1),
                       w_pa[li].astype(bf), w_pb[li].astype(bf), w_out[li].astype(bf),
                       g_mlp[li].reshape(1, d), w_ff1[li].astype(bf), w_ff2[li].astype(bf),
                       g_ple[li].reshape(1, d), w_ple_gate[li].astype(bf), w_ple[li].astype(bf),
                       g_final.reshape(1, d), li == depth - 1)
    return h.reshape(batch, seq, d)
```

```python
import functools

import jax
import jax.numpy as jnp
import numpy as np
from jax import lax
from jax.experimental import pallas as pl
from jax.experimental.pallas import tpu as pltpu

D_MODEL = 1024
N_HEADS_A = 8
N_KV_HEADS_A = 2
HEAD_DIM_A = 64
GROUP_A = N_HEADS_A // N_KV_HEADS_A
N_IDX_HEADS = 4
IDX_DIM = 64
TOPK_MAX = 256
N_HEADS_B = 8
Q_LORA = 384
KV_LORA = 256
NOPE_DIM = 64
ROPE_DIM = 32
HALF_ROPE = ROPE_DIM // 2
V_DIM_B = 64
ROPE_THETA = 10000.0
D_FF = 4 * D_MODEL
PLE_DIM = 256
EPS = 1e-6

LANE = 128
SUBLANE = 8
HEAD_SLOT = 128
VMEM_LIMIT = 56 * 1024 * 1024

TM_PROJ = 512
TM_POST = 512
TQ = 256
KC = 256
FF_CHUNK = 1024
NEG = -1e30

C_QA = 0
C_QI = C_QA + N_HEADS_A * HEAD_DIM_A
C_KA = C_QI + N_IDX_HEADS * IDX_DIM
C_KI = C_KA + 2 * N_KV_HEADS_A * LANE
C_CQ = C_KI + 2 * LANE
C_CKV = C_CQ + Q_LORA
C_KR = C_CKV + KV_LORA
C_GA = C_KR + LANE
C_GB = C_GA + D_MODEL
C_END = C_GB + D_MODEL

_NT = (((1,), (1,)), ((), ()))
_TN = (((0,), (0,)), ((), ()))


def _rms(x, g):
    return x * lax.rsqrt(jnp.mean(x * x, axis=-1, keepdims=True) + EPS) * g


def _rope_slot(x, cos_t, sin_dn, sin_up):
    up = pltpu.roll(x, HALF_ROPE, 1)
    dn = pltpu.roll(x, LANE - HALF_ROPE, 1)
    return x * cos_t + dn * sin_dn + up * sin_up


def _proj_kernel(x_ref, pos_ref, invl_ref, gmix_ref, win_ref, wvat_ref, wwit_ref, gcq_ref,
                 wuq_ref, gckv_ref, wukvk_ref, wukvvt_ref,
                 qa_ref, qi_ref, qb_ref, wit_ref, kap_ref, kip_ref, kb_ref, vat_ref, vbt_ref,
                 sga_ref, sgb_ref):
    bf = jnp.bfloat16
    n = _rms(x_ref[...], gmix_ref[...]).astype(bf)

    def cols(c0, c1):
        return jnp.dot(n, win_ref[:, c0:c1], preferred_element_type=jnp.float32)

    qa_ref[...] = (cols(C_QA, C_QI) * (HEAD_DIM_A ** -0.5)).astype(bf)
    qi_ref[...] = cols(C_QI, C_KA).astype(bf)
    kap_ref[...] = cols(C_KA, C_KI).astype(bf)
    kip_ref[...] = cols(C_KI, C_CQ).astype(bf)
    sga_ref[...] = jax.nn.sigmoid(cols(C_GA, C_GB)).astype(bf)
    sgb_ref[...] = jax.nn.sigmoid(cols(C_GB, C_END)).astype(bf)
    def put_chunks(ref, vt):
        for j in range(TM_PROJ // KC):
            ref[j] = vt[:, j * KC:(j + 1) * KC].astype(bf)

    put_chunks(vat_ref, lax.dot_general(wvat_ref[...], n, _NT, preferred_element_type=jnp.float32))
    wit_ref[...] = lax.dot_general(wwit_ref[...], n, _NT,
                                   preferred_element_type=jnp.float32) * ((N_IDX_HEADS * IDX_DIM) ** -0.5)

    lane = lax.broadcasted_iota(jnp.int32, (1, LANE), 1)
    is_x1 = (lane >= NOPE_DIM) & (lane < NOPE_DIM + HALF_ROPE)
    is_x2 = (lane >= NOPE_DIM + HALF_ROPE) & (lane < NOPE_DIM + ROPE_DIM)
    ang = pos_ref[...] * invl_ref[...]
    cos, sin = jnp.cos(ang), jnp.sin(ang)
    cos_t = jnp.where(is_x1 | is_x2, cos, 1.0)
    sin_dn = jnp.where(is_x1, -sin, 0.0)
    sin_up = jnp.where(is_x2, sin, 0.0)

    cqn = _rms(cols(C_CQ, C_CKV), gcq_ref[...]).astype(bf)
    qscale = (NOPE_DIM + ROPE_DIM) ** -0.5
    for h in range(N_HEADS_B):
        sl = slice(h * HEAD_SLOT, (h + 1) * HEAD_SLOT)
        qh = jnp.dot(cqn, wuq_ref[:, sl], preferred_element_type=jnp.float32)
        qb_ref[:, sl] = (_rope_slot(qh, cos_t, sin_dn, sin_up) * qscale).astype(bf)

    ckvn = _rms(cols(C_CKV, C_KR), gckv_ref[...]).astype(bf)
    kr = _rope_slot(cols(C_KR, C_GA), cos_t, sin_dn, sin_up)
    for h in range(N_HEADS_B):
        sl = slice(h * HEAD_SLOT, (h + 1) * HEAD_SLOT)
        kh = jnp.dot(ckvn, wukvk_ref[:, sl], preferred_element_type=jnp.float32)
        kb_ref[:, sl] = (kh + kr).astype(bf)
    put_chunks(vbt_ref, lax.dot_general(wukvvt_ref[...], ckvn, _NT,
                                        preferred_element_type=jnp.float32))


def _const_spec(shape):
    return pl.BlockSpec(shape, lambda *_: (0,) * len(shape))


def _proj_call(x2, posc, invl, gmix, win, wvat, wwit, gcq, wuq, gckv, wukvk, wukvvt):
    n_tok = x2.shape[0]
    tm = TM_PROJ
    bf = jnp.bfloat16
    row = lambda w: pl.BlockSpec((tm, w), lambda t: (t, 0))
    col = lambda r: pl.BlockSpec((r, tm), lambda t: (0, t))
    chunks = lambda r: pl.BlockSpec((tm // KC, r, KC), lambda t: (t, 0, 0))
    in_specs = [row(D_MODEL), row(1), _const_spec(invl.shape), _const_spec(gmix.shape),
                _const_spec(win.shape), _const_spec(wvat.shape), _const_spec(wwit.shape),
                _const_spec(gcq.shape), _const_spec(wuq.shape), _const_spec(gckv.shape),
                _const_spec(wukvk.shape), _const_spec(wukvvt.shape)]
    out_shape = [
        jax.ShapeDtypeStruct((n_tok, C_QI - C_QA), bf),
        jax.ShapeDtypeStruct((n_tok, C_KA - C_QI), bf),
        jax.ShapeDtypeStruct((n_tok, N_HEADS_B * HEAD_SLOT), bf),
        jax.ShapeDtypeStruct((SUBLANE, n_tok), jnp.float32),
        jax.ShapeDtypeStruct((n_tok, C_KI - C_KA), bf),
        jax.ShapeDtypeStruct((n_tok, C_CQ - C_KI), bf),
        jax.ShapeDtypeStruct((n_tok, N_HEADS_B * HEAD_SLOT), bf),
        jax.ShapeDtypeStruct((n_tok // KC, N_KV_HEADS_A * HEAD_DIM_A, KC), bf),
        jax.ShapeDtypeStruct((n_tok // KC, N_HEADS_B * V_DIM_B, KC), bf),
        jax.ShapeDtypeStruct((n_tok, D_MODEL), bf),
        jax.ShapeDtypeStruct((n_tok, D_MODEL), bf),
    ]
    out_specs = [row(C_QI - C_QA), row(C_KA - C_QI), row(N_HEADS_B * HEAD_SLOT), col(SUBLANE),
                 row(C_KI - C_KA), row(C_CQ - C_KI), row(N_HEADS_B * HEAD_SLOT),
                 chunks(N_KV_HEADS_A * HEAD_DIM_A), chunks(N_HEADS_B * V_DIM_B),
                 row(D_MODEL), row(D_MODEL)]
    return pl.pallas_call(
        _proj_kernel,
        grid=(n_tok // tm,),
        in_specs=in_specs,
        out_specs=out_specs,
        out_shape=out_shape,
        compiler_params=pltpu.CompilerParams(
            dimension_semantics=("arbitrary",), vmem_limit_bytes=VMEM_LIMIT),
        name="proj",
    )(x2, posc, invl, gmix, win, wvat, wwit, gcq, wuq, gckv, wukvk, wukvvt)


def _col_sum(a):
    r = a.shape[0]
    part = jnp.sum(a.reshape(r // SUBLANE, SUBLANE, a.shape[1]), axis=0)
    return jnp.sum(part, axis=0, keepdims=True)


def _col_max(a):
    r = a.shape[0]
    part = jnp.max(a.reshape(r // SUBLANE, SUBLANE, a.shape[1]), axis=0)
    return jnp.max(part, axis=0, keepdims=True)


def _attn_kernel(qa_ref, qi_ref, qb_ref, wit_ref, posq_ref,
                 kap_ref, kip_ref, kb_ref, vat_ref, vbt_ref, posk_ref,
                 yat_ref, ybt_ref,
                 score_ref, lg_ref, dist_ref, acc_ref, m_ref, mu_ref, l_ref, *, topk):
    bf = jnp.bfloat16
    f32 = jnp.float32
    i = pl.program_id(1)
    nch = i + 1
    row_i = lax.broadcasted_iota(jnp.int32, (KC, TQ), 0)
    col_i = lax.broadcasted_iota(jnp.int32, (KC, TQ), 1)
    causal = row_i <= col_i
    kf = float(topk)

    def kslice(c):
        if isinstance(c, int):
            return pl.ds(c * KC, KC)
        return pl.ds(pl.multiple_of(c * KC, KC), KC)

    w = wit_ref[...]

    def score_chunk(c, carry):
        acc = jnp.zeros((KC, TQ), f32)
        for h in range(N_IDX_HEADS):
            kk = kip_ref[kslice(c), (h % 2) * LANE:(h % 2 + 1) * LANE]
            qq = qi_ref[:, (h // 2) * LANE:(h // 2 + 1) * LANE]
            d = lax.dot_general(kk, qq, _NT, preferred_element_type=f32)
            acc = acc + jnp.maximum(d, 0.0) * w[h:h + 1, :]
        score_ref[c] = acc
        return carry

    lax.fori_loop(0, nch, score_chunk, 0)
    score_ref[i] = jnp.where(causal, score_ref[i], -jnp.inf)

    def count(pred):
        def body(c, acc):
            mk = jnp.where(pred(score_ref[c]), 1.0, 0.0)
            return acc + jnp.sum(mk.reshape(KC // SUBLANE, SUBLANE, TQ), axis=0)
        acc = lax.fori_loop(0, nch, body, jnp.zeros((SUBLANE, TQ), f32))
        return jnp.sum(acc, axis=0, keepdims=True)

    def key_to_float(key):
        bits = jnp.where(key >= 0, key, key ^ jnp.int32(0x7FFFFFFF))
        return lax.bitcast_convert_type(bits, f32)

    def search_step(j, key):
        cand = key ^ lax.shift_left(jnp.int32(1), jnp.int32(31) - j)
        thr = key_to_float(cand)
        cnt = count(lambda s: s >= thr)
        return jnp.where(cnt >= kf, cand, key)

    key0 = jnp.full((1, TQ), jnp.iinfo(jnp.int32).min, jnp.int32)
    key = lax.fori_loop(0, 32, search_step, key0)
    t_glob = i * TQ + lax.broadcasted_iota(jnp.int32, (1, TQ), 1)
    take_all = t_glob < topk
    thr = jnp.where(take_all, -jnp.inf, key_to_float(key))
    n_gt = count(lambda s: s > thr)
    need = jnp.where(take_all, 0.0, kf - n_gt)

    tri = jnp.where(lax.broadcasted_iota(jnp.int32, (KC, KC), 0)
                    >= lax.broadcasted_iota(jnp.int32, (KC, KC), 1), 1.0, 0.0).astype(bf)
    posq = posq_ref[...]

    def flash_pass(n_heads, chunk_of, logits_fn, before_logits, values_fn, out_ref, dv):
        m_ref[...] = jnp.full(m_ref.shape, NEG, f32)
        mu_ref[...] = jnp.full(mu_ref.shape, NEG, f32)
        l_ref[...] = jnp.zeros(l_ref.shape, f32)
        acc_ref[...] = jnp.zeros(acc_ref.shape, f32)

        def stage(j, h, first):
            lg = logits_fn(chunk_of(j), h, first)
            lg_ref[h] = lg
            m_ref[h:h + 1, :] = jnp.maximum(m_ref[h:h + 1, :], _col_max(lg))

        def consume(j, h):
            m_now = m_ref[h:h + 1, :]
            alpha = jnp.exp(mu_ref[h:h + 1, :] - m_now)
            mu_ref[h:h + 1, :] = m_now
            p = jnp.exp(lg_ref[h] - m_now)
            l_ref[h:h + 1, :] = alpha * l_ref[h:h + 1, :] + _col_sum(p)
            pv = jnp.dot(values_fn(chunk_of(j), h), p.astype(bf), preferred_element_type=f32)
            acc_ref[h] = alpha * acc_ref[h] + pv

        before_logits(chunk_of(0))
        for h in range(n_heads):
            stage(0, h, True)

        def body(j, carry):
            before_logits(chunk_of(j + 1))
            for h in range(n_heads):
                consume(j, h)
                stage(j + 1, h, False)
            return carry

        lax.fori_loop(0, i, body, 0)
        for h in range(n_heads):
            consume(i, h)
            inv_l = 1.0 / l_ref[h:h + 1, :]
            out_ref[h * dv:(h + 1) * dv, :] = (acc_ref[h] * inv_l).astype(out_ref.dtype)

    def select_chunk(c, taken):
        s = score_ref[c]
        eq = s == thr
        pc = jnp.dot(tri, jnp.where(eq, 1.0, 0.0).astype(bf),
                     preferred_element_type=f32) + taken
        sel = (s > thr) | (eq & (pc <= need))
        score_ref[c] = jnp.where(sel, 0.0, NEG)
        return pc[KC - 1:KC, :]

    lax.fori_loop(0, nch, select_chunk, jnp.zeros((1, TQ), f32))

    def dsa_before(c):
        dist_ref[...] = jnp.abs(posk_ref[kslice(c), :] - posq)

    def dsa_logits(c, h, first):
        g = h // GROUP_A
        kk = kap_ref[kslice(c), (2 * g + h % 2) * LANE:(2 * g + h % 2 + 1) * LANE]
        qq = qa_ref[:, (h // 2) * LANE:(h // 2 + 1) * LANE]
        slope = 2.0 ** (-8.0 * (h + 1) / N_HEADS_A)
        qk = lax.dot_general(kk, qq, _NT, preferred_element_type=f32)
        return qk - slope * dist_ref[...] + score_ref[c]

    def dsa_values(c, h):
        g = h // GROUP_A
        return vat_ref[c, g * HEAD_DIM_A:(g + 1) * HEAD_DIM_A, :]

    flash_pass(N_HEADS_A, lambda j: j, dsa_logits, dsa_before, dsa_values, yat_ref, HEAD_DIM_A)

    def mla_chunk_of(j):
        return jnp.where(j == 0, i, j - 1)

    def mla_logits(c, h, first):
        kk = kb_ref[kslice(c), h * HEAD_SLOT:(h + 1) * HEAD_SLOT]
        qq = qb_ref[:, h * HEAD_SLOT:(h + 1) * HEAD_SLOT]
        qk = lax.dot_general(kk, qq, _NT, preferred_element_type=f32)
        return jnp.where(causal, qk, NEG) if first else qk

    def mla_values(c, h):
        return vbt_ref[c, h * V_DIM_B:(h + 1) * V_DIM_B, :]

    flash_pass(N_HEADS_B, mla_chunk_of, mla_logits, lambda c: None, mla_values, ybt_ref, V_DIM_B)


def _attn_call(qa, qi, qb, wit, posr, kap, kip, kb, vat, vbt, posc, batch, seq, topk):
    nq = seq // TQ
    bf = jnp.bfloat16
    n_tok = batch * seq
    qrow = lambda w: pl.BlockSpec((TQ, w), lambda b, i: (b * nq + i, 0))
    qcol = lambda r: pl.BlockSpec((r, TQ), lambda b, i: (0, b * nq + i))
    krow = lambda w: pl.BlockSpec((seq, w), lambda b, i: (b, 0))
    kchunks = lambda r: pl.BlockSpec((seq // KC, r, KC), lambda b, i: (b, 0, 0))
    in_specs = [qrow(qa.shape[1]), qrow(qi.shape[1]), qrow(qb.shape[1]), qcol(SUBLANE), qcol(1),
                krow(kap.shape[1]), krow(kip.shape[1]), krow(kb.shape[1]),
                kchunks(vat.shape[1]), kchunks(vbt.shape[1]), krow(1)]
    out_shape = [jax.ShapeDtypeStruct((N_HEADS_A * HEAD_DIM_A, n_tok), bf),
                 jax.ShapeDtypeStruct((N_HEADS_B * V_DIM_B, n_tok), bf)]
    out_specs = [qcol(N_HEADS_A * HEAD_DIM_A), qcol(N_HEADS_B * V_DIM_B)]
    scratch = [pltpu.VMEM((seq // KC, KC, TQ), jnp.float32),
               pltpu.VMEM((N_HEADS_A, KC, TQ), jnp.float32),
               pltpu.VMEM((KC, TQ), jnp.float32),
               pltpu.VMEM((N_HEADS_A, HEAD_DIM_A, TQ), jnp.float32),
               pltpu.VMEM((N_HEADS_A, TQ), jnp.float32),
               pltpu.VMEM((N_HEADS_A, TQ), jnp.float32),
               pltpu.VMEM((N_HEADS_A, TQ), jnp.float32)]
    return pl.pallas_call(
        functools.partial(_attn_kernel, topk=topk),
        grid=(batch, nq),
        in_specs=in_specs,
        out_specs=out_specs,
        out_shape=out_shape,
        scratch_shapes=scratch,
        compiler_params=pltpu.CompilerParams(
            dimension_semantics=("arbitrary", "arbitrary"), vmem_limit_bytes=VMEM_LIMIT),
        name="attn",
    )(qa, qi, qb, wit, posr, kap, kip, kb, vat, vbt, posc)


def _post_kernel(x_ref, yat_ref, ybt_ref, sga_ref, sgb_ref, p_ref,
                 wpa_ref, wpb_ref, wout_ref, gmlp_ref, wff1_ref, wff2_ref, gple_ref,
                 wpg_ref, wple_ref, gfin_ref, o_ref, *, final_norm):
    bf = jnp.bfloat16
    f32 = jnp.float32
    a = lax.dot_general(yat_ref[...], wpa_ref[...], _TN, preferred_element_type=f32)
    b = lax.dot_general(ybt_ref[...], wpb_ref[...], _TN, preferred_element_type=f32)
    merged = sga_ref[...].astype(f32) * a + sgb_ref[...].astype(f32) * b
    h = x_ref[...] + jnp.dot(merged.astype(bf), wout_ref[...], preferred_element_type=f32)
    n2 = _rms(h, gmlp_ref[...]).astype(bf)
    ff = jnp.zeros_like(h)
    for c in range(D_FF // FF_CHUNK):
        sl = slice(c * FF_CHUNK, (c + 1) * FF_CHUNK)
        u = jnp.maximum(jnp.dot(n2, wff1_ref[:, sl], preferred_element_type=f32), 0.0)
        ff = ff + jnp.dot((u * u).astype(bf), wff2_ref[sl, :], preferred_element_type=f32)
    h = h + ff
    n3 = _rms(h, gple_ref[...]).astype(bf)
    gate = jax.nn.sigmoid(jnp.dot(n3, wpg_ref[...], preferred_element_type=f32))
    pe = jnp.dot(p_ref[...].astype(bf), wple_ref[...], preferred_element_type=f32)
    h = h + gate * pe
    o_ref[...] = _rms(h, gfin_ref[...]) if final_norm else h


def _post_call(x2, yat, ybt, sga, sgb, p2, wpa, wpb, wout, gmlp, wff1, wff2, gple, wpg, wple, gfin,
               final_norm):
    n_tok = x2.shape[0]
    tm = TM_POST
    row = lambda w: pl.BlockSpec((tm, w), lambda t: (t, 0))
    col = lambda r: pl.BlockSpec((r, tm), lambda t: (0, t))
    wspec = lambda a: pl.BlockSpec(a.shape, lambda t: (0,) * a.ndim, pipeline_mode=pl.Buffered(1))
    in_specs = [row(D_MODEL), col(yat.shape[0]), col(ybt.shape[0]), row(D_MODEL), row(D_MODEL),
                row(PLE_DIM)] + [wspec(a) for a in
                                 (wpa, wpb, wout, gmlp, wff1, wff2, gple, wpg, wple, gfin)]
    return pl.pallas_call(
        functools.partial(_post_kernel, final_norm=final_norm),
        grid=(n_tok // tm,),
        in_specs=in_specs,
        out_specs=row(D_MODEL),
        out_shape=jax.ShapeDtypeStruct((n_tok, D_MODEL), jnp.float32),
        compiler_params=pltpu.CompilerParams(
            dimension_semantics=("arbitrary",), vmem_limit_bytes=VMEM_LIMIT),
        name="post",
    )(x2, yat, ybt, sga, sgb, p2, wpa, wpb, wout, gmlp, wff1, wff2, gple, wpg, wple, gfin)


def _pack_layer(w_in, w_uq, w_ukv):
    bf = jnp.bfloat16
    sizes = (N_HEADS_A * HEAD_DIM_A, N_KV_HEADS_A * HEAD_DIM_A, N_KV_HEADS_A * HEAD_DIM_A,
             N_IDX_HEADS * IDX_DIM, IDX_DIM, N_IDX_HEADS, Q_LORA, KV_LORA, ROPE_DIM,
             D_MODEL, D_MODEL)
    offs = np.concatenate([[0], np.cumsum(sizes)])
    wqa, wka, wva, wqi, wki, wwi, wcq, wckv, wkr, wga, wgb = [
        w_in[:, offs[j]:offs[j + 1]] for j in range(len(sizes))]
    d = w_in.shape[0]
    z64 = jnp.zeros((d, HEAD_DIM_A), w_in.dtype)
    ka_cols = []
    for g in range(N_KV_HEADS_A):
        kg = wka[:, g * HEAD_DIM_A:(g + 1) * HEAD_DIM_A]
        ka_cols += [kg, z64, z64, kg]
    ki_cols = [wki, z64, z64, wki]
    kr_cols = [jnp.zeros((d, NOPE_DIM), w_in.dtype), wkr,
               jnp.zeros((d, HEAD_SLOT - NOPE_DIM - ROPE_DIM), w_in.dtype)]
    win = jnp.concatenate([wqa, wqi] + ka_cols + ki_cols + [wcq, wckv] + kr_cols + [wga, wgb],
                          axis=1).astype(bf)
    wvat = wva.T.astype(bf)
    wwit = jnp.concatenate([wwi.T, jnp.zeros((SUBLANE - N_IDX_HEADS, d), w_in.dtype)],
                           axis=0).astype(bf)
    uq = w_uq.reshape(Q_LORA, N_HEADS_B, NOPE_DIM + ROPE_DIM)
    uq = jnp.pad(uq, ((0, 0), (0, 0), (0, HEAD_SLOT - NOPE_DIM - ROPE_DIM)))
    wuq = uq.reshape(Q_LORA, N_HEADS_B * HEAD_SLOT).astype(bf)
    ukv = w_ukv.reshape(KV_LORA, N_HEADS_B, NOPE_DIM + V_DIM_B)
    uk = jnp.pad(ukv[:, :, :NOPE_DIM], ((0, 0), (0, 0), (0, HEAD_SLOT - NOPE_DIM)))
    wukvk = uk.reshape(KV_LORA, N_HEADS_B * HEAD_SLOT).astype(bf)
    wukvvt = ukv[:, :, NOPE_DIM:].reshape(KV_LORA, N_HEADS_B * V_DIM_B).T.astype(bf)
    return win, wvat, wwit, wuq, wukvk, wukvvt


def kernel(x, p, positions, g_mix, w_in, g_cq, w_uq, g_ckv, w_ukv, w_pa, w_pb, w_out, g_mlp,
           w_ff1, w_ff2, g_ple, w_ple_gate, w_ple, g_final):
    bf = jnp.bfloat16
    batch, seq, d = x.shape
    depth = w_in.shape[0]
    n_tok = batch * seq
    topk = min(TOPK_MAX, seq // 4)
    assert d == D_MODEL and seq % TQ == 0 and TQ == KC and topk <= TQ and n_tok % TM_PROJ == 0

    posf = positions.astype(jnp.float32)
    posc = posf.reshape(n_tok, 1)
    posr = posf.reshape(1, n_tok)
    inv = ROPE_THETA ** (-jnp.arange(HALF_ROPE, dtype=jnp.float32) / HALF_ROPE)
    invl = jnp.concatenate([jnp.zeros((NOPE_DIM,), jnp.float32), inv, inv,
                            jnp.zeros((HEAD_SLOT - NOPE_DIM - ROPE_DIM,), jnp.float32)]).reshape(1, LANE)

    h = x.reshape(n_tok, d)
    for li in range(depth):
        win, wvat, wwit, wuq, wukvk, wukvvt = _pack_layer(w_in[li], w_uq[li], w_ukv[li])
        (qa, qi, qb, wit, kap, kip, kb, vat, vbt, sga, sgb) = _proj_call(
            h, posc, invl, g_mix[li].reshape(1, d), win, wvat, wwit, g_cq[li].reshape(1, -1),
            wuq, g_ckv[li].reshape(1, -1), wukvk, wukvvt)
        yat, ybt = _attn_call(qa, qi, qb, wit, posr, kap, kip, kb, vat, vbt, posc, batch, seq, topk)
        h = _post_call(h, yat, ybt, sga, sgb, p[li].reshape(n_tok, -1),
                       w_pa[li].astype(bf), w_pb[li].astype(bf), w_out[li].astype(bf),
                       g_mlp[li].reshape(1, d), w_ff1[li].astype(bf), w_ff2[li].astype(bf),
                       g_ple[li].reshape(1, d), w_ple_gate[li].astype(bf), w_ple[li].astype(bf),
                       g_final.reshape(1, d), li == depth - 1)
    return h.reshape(batch, seq, d)
```

```python
import functools

import jax
import jax.numpy as jnp
import numpy as np
from jax import lax
from jax.experimental import pallas as pl
from jax.experimental.pallas import tpu as pltpu

D_MODEL = 1024
N_HEADS_A = 8
N_KV_HEADS_A = 2
HEAD_DIM_A = 64
GROUP_A = N_HEADS_A // N_KV_HEADS_A
N_IDX_HEADS = 4
IDX_DIM = 64
TOPK_MAX = 256
N_HEADS_B = 8
Q_LORA = 384
KV_LORA = 256
NOPE_DIM = 64
ROPE_DIM = 32
HALF_ROPE = ROPE_DIM // 2
V_DIM_B = 64
ROPE_THETA = 10000.0
D_FF = 4 * D_MODEL
PLE_DIM = 256
EPS = 1e-6

LANE = 128
SUBLANE = 8
HEAD_SLOT = 128
VMEM_LIMIT = 56 * 1024 * 1024

TM_PROJ = 512
TM_POST = 512
TQ = 256
KC = 256
FF_CHUNK = 1024
NEG = -1e30
FAR = 1e30
LOG2E = 1.4426950408889634
COUNT_ROWS = 32

C_QA = 0
C_QI = C_QA + N_HEADS_A * HEAD_DIM_A
C_KA = C_QI + N_IDX_HEADS * IDX_DIM
C_KI = C_KA + 2 * N_KV_HEADS_A * LANE
C_CQ = C_KI + 2 * LANE
C_CKV = C_CQ + Q_LORA
C_KR = C_CKV + KV_LORA
C_GA = C_KR + LANE
C_GB = C_GA + D_MODEL
C_END = C_GB + D_MODEL

_NT = (((1,), (1,)), ((), ()))
_TN = (((0,), (0,)), ((), ()))


def _rms(x, g):
    return x * lax.rsqrt(jnp.mean(x * x, axis=-1, keepdims=True) + EPS) * g


def _rope_slot(x, cos_t, sin_dn, sin_up):
    up = pltpu.roll(x, HALF_ROPE, 1)
    dn = pltpu.roll(x, LANE - HALF_ROPE, 1)
    return x * cos_t + dn * sin_dn + up * sin_up


def _proj_kernel(x_ref, pos_ref, invl_ref, gmix_ref, win_ref, wvat_ref, wwit_ref, gcq_ref,
                 wuq_ref, gckv_ref, wukvk_ref, wukvvt_ref,
                 qa_ref, qi_ref, qb_ref, wit_ref, kap_ref, kip_ref, kb_ref, vat_ref, vbt_ref,
                 sga_ref, sgb_ref):
    bf = jnp.bfloat16
    n = _rms(x_ref[...], gmix_ref[...]).astype(bf)

    def cols(c0, c1):
        return jnp.dot(n, win_ref[:, c0:c1], preferred_element_type=jnp.float32)

    qa_ref[...] = (cols(C_QA, C_QI) * (HEAD_DIM_A ** -0.5 * LOG2E)).astype(bf)
    qi_ref[...] = cols(C_QI, C_KA).astype(bf)
    kap_ref[...] = cols(C_KA, C_KI).astype(bf)
    kip_ref[...] = cols(C_KI, C_CQ).astype(bf)
    sga_ref[...] = jax.nn.sigmoid(cols(C_GA, C_GB)).astype(bf)
    sgb_ref[...] = jax.nn.sigmoid(cols(C_GB, C_END)).astype(bf)
    def put_chunks(ref, vt):
        for j in range(TM_PROJ // KC):
            ref[j] = vt[:, j * KC:(j + 1) * KC].astype(bf)

    put_chunks(vat_ref, lax.dot_general(wvat_ref[...], n, _NT, preferred_element_type=jnp.float32))
    wit_ref[...] = lax.dot_general(wwit_ref[...], n, _NT,
                                   preferred_element_type=jnp.float32) * ((N_IDX_HEADS * IDX_DIM) ** -0.5)

    lane = lax.broadcasted_iota(jnp.int32, (1, LANE), 1)
    is_x1 = (lane >= NOPE_DIM) & (lane < NOPE_DIM + HALF_ROPE)
    is_x2 = (lane >= NOPE_DIM + HALF_ROPE) & (lane < NOPE_DIM + ROPE_DIM)
    ang = pos_ref[...] * invl_ref[...]
    cos, sin = jnp.cos(ang), jnp.sin(ang)
    cos_t = jnp.where(is_x1 | is_x2, cos, 1.0)
    sin_dn = jnp.where(is_x1, -sin, 0.0)
    sin_up = jnp.where(is_x2, sin, 0.0)

    cqn = _rms(cols(C_CQ, C_CKV), gcq_ref[...]).astype(bf)
    qscale = (NOPE_DIM + ROPE_DIM) ** -0.5 * LOG2E
    qall = jnp.dot(cqn, wuq_ref[...], preferred_element_type=jnp.float32)
    for h in range(N_HEADS_B):
        sl = slice(h * HEAD_SLOT, (h + 1) * HEAD_SLOT)
        qb_ref[:, sl] = (_rope_slot(qall[:, sl], cos_t, sin_dn, sin_up) * qscale).astype(bf)

    ckvn = _rms(cols(C_CKV, C_KR), gckv_ref[...]).astype(bf)
    kr = _rope_slot(cols(C_KR, C_GA), cos_t, sin_dn, sin_up)
    kall = jnp.dot(ckvn, wukvk_ref[...], preferred_element_type=jnp.float32)
    for h in range(N_HEADS_B):
        sl = slice(h * HEAD_SLOT, (h + 1) * HEAD_SLOT)
        kb_ref[:, sl] = (kall[:, sl] + kr).astype(bf)
    put_chunks(vbt_ref, lax.dot_general(wukvvt_ref[...], ckvn, _NT,
                                        preferred_element_type=jnp.float32))


def _const_spec(shape):
    return pl.BlockSpec(shape, lambda *_: (0,) * len(shape))


def _proj_call(x2, posc, invl, gmix, win, wvat, wwit, gcq, wuq, gckv, wukvk, wukvvt):
    n_tok = x2.shape[0]
    tm = TM_PROJ
    bf = jnp.bfloat16
    row = lambda w: pl.BlockSpec((tm, w), lambda t: (t, 0))
    col = lambda r: pl.BlockSpec((r, tm), lambda t: (0, t))
    chunks = lambda r: pl.BlockSpec((tm // KC, r, KC), lambda t: (t, 0, 0))
    in_specs = [row(D_MODEL), row(1), _const_spec(invl.shape), _const_spec(gmix.shape),
                _const_spec(win.shape), _const_spec(wvat.shape), _const_spec(wwit.shape),
                _const_spec(gcq.shape), _const_spec(wuq.shape), _const_spec(gckv.shape),
                _const_spec(wukvk.shape), _const_spec(wukvvt.shape)]
    out_shape = [
        jax.ShapeDtypeStruct((n_tok, C_QI - C_QA), bf),
        jax.ShapeDtypeStruct((n_tok, C_KA - C_QI), bf),
        jax.ShapeDtypeStruct((n_tok, N_HEADS_B * HEAD_SLOT), bf),
        jax.ShapeDtypeStruct((SUBLANE, n_tok), jnp.float32),
        jax.ShapeDtypeStruct((n_tok, C_KI - C_KA), bf),
        jax.ShapeDtypeStruct((n_tok, C_CQ - C_KI), bf),
        jax.ShapeDtypeStruct((n_tok, N_HEADS_B * HEAD_SLOT), bf),
        jax.ShapeDtypeStruct((n_tok // KC, N_KV_HEADS_A * HEAD_DIM_A, KC), bf),
        jax.ShapeDtypeStruct((n_tok // KC, N_HEADS_B * V_DIM_B, KC), bf),
        jax.ShapeDtypeStruct((n_tok, D_MODEL), bf),
        jax.ShapeDtypeStruct((n_tok, D_MODEL), bf),
    ]
    out_specs = [row(C_QI - C_QA), row(C_KA - C_QI), row(N_HEADS_B * HEAD_SLOT), col(SUBLANE),
                 row(C_KI - C_KA), row(C_CQ - C_KI), row(N_HEADS_B * HEAD_SLOT),
                 chunks(N_KV_HEADS_A * HEAD_DIM_A), chunks(N_HEADS_B * V_DIM_B),
                 row(D_MODEL), row(D_MODEL)]
    return pl.pallas_call(
        _proj_kernel,
        grid=(n_tok // tm,),
        in_specs=in_specs,
        out_specs=out_specs,
        out_shape=out_shape,
        compiler_params=pltpu.CompilerParams(
            dimension_semantics=("arbitrary",), vmem_limit_bytes=VMEM_LIMIT),
        name="proj",
    )(x2, posc, invl, gmix, win, wvat, wwit, gcq, wuq, gckv, wukvk, wukvvt)


def _col_sum(a):
    r = a.shape[0]
    part = jnp.sum(a.reshape(r // SUBLANE, SUBLANE, a.shape[1]), axis=0)
    return jnp.sum(part, axis=0, keepdims=True)


def _col_max(a):
    r = a.shape[0]
    part = jnp.max(a.reshape(r // SUBLANE, SUBLANE, a.shape[1]), axis=0)
    return jnp.max(part, axis=0, keepdims=True)


def _attn_kernel(qa_ref, qi_ref, qb_ref, wit_ref, posq_ref,
                 kap_ref, kip_ref, kb_ref, vat_ref, vbt_ref, posk_ref,
                 yat_ref, ybt_ref,
                 score_ref, lg_ref, acc_ref, m_ref, mu_ref, l_ref, *, topk):
    bf = jnp.bfloat16
    f32 = jnp.float32
    i = pl.program_id(1)
    nch = i + 1
    row_i = lax.broadcasted_iota(jnp.int32, (KC, TQ), 0)
    col_i = lax.broadcasted_iota(jnp.int32, (KC, TQ), 1)
    causal = row_i <= col_i
    kf = float(topk)

    def kslice(c):
        if isinstance(c, int):
            return pl.ds(c * KC, KC)
        return pl.ds(pl.multiple_of(c * KC, KC), KC)

    w = wit_ref[...]

    def score_chunk(c, carry):
        acc = jnp.zeros((KC, TQ), f32)
        for h in range(N_IDX_HEADS):
            kk = kip_ref[kslice(c), (h % 2) * LANE:(h % 2 + 1) * LANE]
            qq = qi_ref[:, (h // 2) * LANE:(h // 2 + 1) * LANE]
            d = lax.dot_general(kk, qq, _NT, preferred_element_type=f32)
            acc = acc + jnp.maximum(d, 0.0) * w[h:h + 1, :]
        score_ref[c] = acc
        return carry

    lax.fori_loop(0, nch, score_chunk, 0)
    score_ref[i] = jnp.where(causal, score_ref[i], -jnp.inf)

    def count(pred):
        def body(c, acc):
            mk = jnp.where(pred(score_ref[c]), 1.0, 0.0)
            return acc + jnp.sum(mk.reshape(KC // COUNT_ROWS, COUNT_ROWS, TQ), axis=0)
        acc = lax.fori_loop(0, nch, body, jnp.zeros((COUNT_ROWS, TQ), f32))
        return _col_sum(acc)

    def key_to_float(key):
        bits = jnp.where(key >= 0, key, key ^ jnp.int32(0x7FFFFFFF))
        return lax.bitcast_convert_type(bits, f32)

    def search_step(j, key):
        cand = key ^ lax.shift_left(jnp.int32(1), jnp.int32(31) - j)
        thr_c = key_to_float(cand)
        cnt = count(lambda s: s >= thr_c)
        return jnp.where(cnt >= kf, cand, key)

    key0 = jnp.full((1, TQ), jnp.iinfo(jnp.int32).min, jnp.int32)
    key = lax.fori_loop(0, 32, search_step, key0)
    t_glob = i * TQ + lax.broadcasted_iota(jnp.int32, (1, TQ), 1)
    take_all = t_glob < topk
    thr = jnp.where(take_all, -jnp.inf, key_to_float(key))
    n_gt = count(lambda s: s > thr)
    need = jnp.where(take_all, 0.0, kf - n_gt)

    tri = jnp.where(lax.broadcasted_iota(jnp.int32, (KC, KC), 0)
                    >= lax.broadcasted_iota(jnp.int32, (KC, KC), 1), 1.0, 0.0).astype(bf)
    posq = posq_ref[...]

    def select_chunk(c, taken):
        s = score_ref[c]
        eq = s == thr
        pc = jnp.dot(tri, jnp.where(eq, 1.0, 0.0).astype(bf),
                     preferred_element_type=f32) + taken
        sel = (s > thr) | (eq & (pc <= need))
        dist = jnp.abs(posk_ref[kslice(c), :] - posq)
        score_ref[c] = jnp.where(sel, dist, FAR)
        return pc[KC - 1:KC, :]

    lax.fori_loop(0, nch, select_chunk, jnp.zeros((1, TQ), f32))

    def dsa_logits(c, h, first):
        g = h // GROUP_A
        kk = kap_ref[kslice(c), (2 * g + h % 2) * LANE:(2 * g + h % 2 + 1) * LANE]
        qq = qa_ref[:, (h // 2) * LANE:(h // 2 + 1) * LANE]
        slope = 2.0 ** (-8.0 * (h + 1) / N_HEADS_A) * LOG2E
        qk = lax.dot_general(kk, qq, _NT, preferred_element_type=f32)
        return qk - slope * score_ref[c]

    def dsa_values(c, h):
        g = h // GROUP_A
        return vat_ref[c, g * HEAD_DIM_A:(g + 1) * HEAD_DIM_A, :]

    def mla_chunk_of(j):
        return jnp.where(j == 0, i, j - 1)

    def mla_logits(c, h, first):
        kk = kb_ref[kslice(c), h * HEAD_SLOT:(h + 1) * HEAD_SLOT]
        qq = qb_ref[:, h * HEAD_SLOT:(h + 1) * HEAD_SLOT]
        qk = lax.dot_general(kk, qq, _NT, preferred_element_type=f32)
        return jnp.where(causal, qk, NEG) if first else qk

    def mla_values(c, h):
        return vbt_ref[c, h * V_DIM_B:(h + 1) * V_DIM_B, :]

    n_slots = N_HEADS_A + N_HEADS_B
    m_ref[...] = jnp.full(m_ref.shape, NEG, f32)
    mu_ref[...] = jnp.full(mu_ref.shape, NEG, f32)
    l_ref[...] = jnp.zeros(l_ref.shape, f32)
    acc_ref[...] = jnp.zeros(acc_ref.shape, f32)

    def slot_logits(j, s, first):
        if s < N_HEADS_A:
            return dsa_logits(j, s, first)
        return mla_logits(mla_chunk_of(j), s - N_HEADS_A, first)

    def slot_values(j, s):
        if s < N_HEADS_A:
            return dsa_values(j, s)
        return mla_values(mla_chunk_of(j), s - N_HEADS_A)

    def stage(j, s, first):
        lg = slot_logits(j, s, first)
        lg_ref[s] = lg
        m_ref[s:s + 1, :] = jnp.maximum(m_ref[s:s + 1, :], _col_max(lg))

    def consume(j, s):
        m_now = m_ref[s:s + 1, :]
        alpha = jnp.exp2(mu_ref[s:s + 1, :] - m_now)
        mu_ref[s:s + 1, :] = m_now
        p = jnp.exp2(lg_ref[s] - m_now)
        l_ref[s:s + 1, :] = alpha * l_ref[s:s + 1, :] + _col_sum(p)
        pv = jnp.dot(slot_values(j, s), p.astype(bf), preferred_element_type=f32)
        acc_ref[s] = alpha * acc_ref[s] + pv

    for s in range(n_slots):
        stage(0, s, True)

    def sweep(j, carry):
        for s in range(n_slots):
            consume(j, s)
            stage(j + 1, s, False)
        return carry

    lax.fori_loop(0, i, sweep, 0)
    for s in range(n_slots):
        consume(i, s)
        out_ref, h, dv = (yat_ref, s, HEAD_DIM_A) if s < N_HEADS_A else (ybt_ref, s - N_HEADS_A, V_DIM_B)
        inv_l = 1.0 / l_ref[s:s + 1, :]
        out_ref[h * dv:(h + 1) * dv, :] = (acc_ref[s] * inv_l).astype(out_ref.dtype)


def _attn_call(qa, qi, qb, wit, posr, kap, kip, kb, vat, vbt, posc, batch, seq, topk):
    nq = seq // TQ
    bf = jnp.bfloat16
    n_tok = batch * seq
    n_slots = N_HEADS_A + N_HEADS_B
    assert HEAD_DIM_A == V_DIM_B
    qrow = lambda w: pl.BlockSpec((TQ, w), lambda b, i: (b * nq + i, 0))
    qcol = lambda r: pl.BlockSpec((r, TQ), lambda b, i: (0, b * nq + i))
    krow = lambda w: pl.BlockSpec((seq, w), lambda b, i: (b, 0))
    kchunks = lambda r: pl.BlockSpec((seq // KC, r, KC), lambda b, i: (b, 0, 0))
    in_specs = [qrow(qa.shape[1]), qrow(qi.shape[1]), qrow(qb.shape[1]), qcol(SUBLANE), qcol(1),
                krow(kap.shape[1]), krow(kip.shape[1]), krow(kb.shape[1]),
                kchunks(vat.shape[1]), kchunks(vbt.shape[1]), krow(1)]
    out_shape = [jax.ShapeDtypeStruct((N_HEADS_A * HEAD_DIM_A, n_tok), bf),
                 jax.ShapeDtypeStruct((N_HEADS_B * V_DIM_B, n_tok), bf)]
    out_specs = [qcol(N_HEADS_A * HEAD_DIM_A), qcol(N_HEADS_B * V_DIM_B)]
    scratch = [pltpu.VMEM((seq // KC, KC, TQ), jnp.float32),
               pltpu.VMEM((n_slots, KC, TQ), jnp.float32),
               pltpu.VMEM((n_slots, HEAD_DIM_A, TQ), jnp.float32),
               pltpu.VMEM((n_slots, TQ), jnp.float32),
               pltpu.VMEM((n_slots, TQ), jnp.float32),
               pltpu.VMEM((n_slots, TQ), jnp.float32)]
    return pl.pallas_call(
        functools.partial(_attn_kernel, topk=topk),
        grid=(batch, nq),
        in_specs=in_specs,
        out_specs=out_specs,
        out_shape=out_shape,
        scratch_shapes=scratch,
        compiler_params=pltpu.CompilerParams(
            dimension_semantics=("arbitrary", "arbitrary"), vmem_limit_bytes=VMEM_LIMIT),
        name="attn",
    )(qa, qi, qb, wit, posr, kap, kip, kb, vat, vbt, posc)


def _post_kernel(x_ref, yat_ref, ybt_ref, sga_ref, sgb_ref, p_ref,
                 wpa_ref, wpb_ref, wout_ref, gmlp_ref, wff1_ref, wff2_ref, gple_ref,
                 wpg_ref, wple_ref, gfin_ref, o_ref, *, final_norm):
    bf = jnp.bfloat16
    f32 = jnp.float32
    a = lax.dot_general(yat_ref[...], wpa_ref[...], _TN, preferred_element_type=f32)
    b = lax.dot_general(ybt_ref[...], wpb_ref[...], _TN, preferred_element_type=f32)
    merged = sga_ref[...].astype(f32) * a + sgb_ref[...].astype(f32) * b
    h = x_ref[...] + jnp.dot(merged.astype(bf), wout_ref[...], preferred_element_type=f32)
    n2 = _rms(h, gmlp_ref[...]).astype(bf)
    ff = jnp.zeros_like(h)
    for c in range(D_FF // FF_CHUNK):
        sl = slice(c * FF_CHUNK, (c + 1) * FF_CHUNK)
        u = jnp.maximum(jnp.dot(n2, wff1_ref[:, sl], preferred_element_type=f32), 0.0)
        ff = ff + jnp.dot((u * u).astype(bf), wff2_ref[sl, :], preferred_element_type=f32)
    h = h + ff
    n3 = _rms(h, gple_ref[...]).astype(bf)
    gate = jax.nn.sigmoid(jnp.dot(n3, wpg_ref[...], preferred_element_type=f32))
    pe = jnp.dot(p_ref[...].astype(bf), wple_ref[...], preferred_element_type=f32)
    h = h + gate * pe
    o_ref[...] = _rms(h, gfin_ref[...]) if final_norm else h


def _post_call(x2, yat, ybt, sga, sgb, p2, wpa, wpb, wout, gmlp, wff1, wff2, gple, wpg, wple, gfin,
               final_norm):
    n_tok = x2.shape[0]
    tm = TM_POST
    row = lambda w: pl.BlockSpec((tm, w), lambda t: (t, 0))
    col = lambda r: pl.BlockSpec((r, tm), lambda t: (0, t))
    wspec = lambda a: pl.BlockSpec(a.shape, lambda t: (0,) * a.ndim, pipeline_mode=pl.Buffered(1))
    in_specs = [row(D_MODEL), col(yat.shape[0]), col(ybt.shape[0]), row(D_MODEL), row(D_MODEL),
                row(PLE_DIM)] + [wspec(a) for a in
                                 (wpa, wpb, wout, gmlp, wff1, wff2, gple, wpg, wple, gfin)]
    return pl.pallas_call(
        functools.partial(_post_kernel, final_norm=final_norm),
        grid=(n_tok // tm,),
        in_specs=in_specs,
        out_specs=row(D_MODEL),
        out_shape=jax.ShapeDtypeStruct((n_tok, D_MODEL), jnp.float32),
        compiler_params=pltpu.CompilerParams(
            dimension_semantics=("arbitrary",), vmem_limit_bytes=VMEM_LIMIT),
        name="post",
    )(x2, yat, ybt, sga, sgb, p2, wpa, wpb, wout, gmlp, wff1, wff2, gple, wpg, wple, gfin)


def _pack_layer(w_in, w_uq, w_ukv):
    bf = jnp.bfloat16
    sizes = (N_HEADS_A * HEAD_DIM_A, N_KV_HEADS_A * HEAD_DIM_A, N_KV_HEADS_A * HEAD_DIM_A,
             N_IDX_HEADS * IDX_DIM, IDX_DIM, N_IDX_HEADS, Q_LORA, KV_LORA, ROPE_DIM,
             D_MODEL, D_MODEL)
    offs = np.concatenate([[0], np.cumsum(sizes)])
    wqa, wka, wva, wqi, wki, wwi, wcq, wckv, wkr, wga, wgb = [
        w_in[:, offs[j]:offs[j + 1]] for j in range(len(sizes))]
    d = w_in.shape[0]
    z64 = jnp.zeros((d, HEAD_DIM_A), w_in.dtype)
    ka_cols = []
    for g in range(N_KV_HEADS_A):
        kg = wka[:, g * HEAD_DIM_A:(g + 1) * HEAD_DIM_A]
        ka_cols += [kg, z64, z64, kg]
    ki_cols = [wki, z64, z64, wki]
    kr_cols = [jnp.zeros((d, NOPE_DIM), w_in.dtype), wkr,
               jnp.zeros((d, HEAD_SLOT - NOPE_DIM - ROPE_DIM), w_in.dtype)]
    win = jnp.concatenate([wqa, wqi] + ka_cols + ki_cols + [wcq, wckv] + kr_cols + [wga, wgb],
                          axis=1).astype(bf)
    wvat = wva.T.astype(bf)
    wwit = jnp.concatenate([wwi.T, jnp.zeros((SUBLANE - N_IDX_HEADS, d), w_in.dtype)],
                           axis=0).astype(bf)
    uq = w_uq.reshape(Q_LORA, N_HEADS_B, NOPE_DIM + ROPE_DIM)
    uq = jnp.pad(uq, ((0, 0), (0, 0), (0, HEAD_SLOT - NOPE_DIM - ROPE_DIM)))
    wuq = uq.reshape(Q_LORA, N_HEADS_B * HEAD_SLOT).astype(bf)
    ukv = w_ukv.reshape(KV_LORA, N_HEADS_B, NOPE_DIM + V_DIM_B)
    uk = jnp.pad(ukv[:, :, :NOPE_DIM], ((0, 0), (0, 0), (0, HEAD_SLOT - NOPE_DIM)))
    wukvk = uk.reshape(KV_LORA, N_HEADS_B * HEAD_SLOT).astype(bf)
    wukvvt = ukv[:, :, NOPE_DIM:].reshape(KV_LORA, N_HEADS_B * V_DIM_B).T.astype(bf)
    return win, wvat, wwit, wuq, wukvk, wukvvt


def kernel(x, p, positions, g_mix, w_in, g_cq, w_uq, g_ckv, w_ukv, w_pa, w_pb, w_out, g_mlp,
           w_ff1, w_ff2, g_ple, w_ple_gate, w_ple, g_final):
    bf = jnp.bfloat16
    batch, seq, d = x.shape
    depth = w_in.shape[0]
    n_tok = batch * seq
    topk = min(TOPK_MAX, seq // 4)
    assert d == D_MODEL and seq % TQ == 0 and TQ == KC and topk <= TQ and n_tok % TM_PROJ == 0

    posf = positions.astype(jnp.float32)
    posc = posf.reshape(n_tok, 1)
    posr = posf.reshape(1, n_tok)
    inv = ROPE_THETA ** (-jnp.arange(HALF_ROPE, dtype=jnp.float32) / HALF_ROPE)
    invl = jnp.concatenate([jnp.zeros((NOPE_DIM,), jnp.float32), inv, inv,
                            jnp.zeros((HEAD_SLOT - NOPE_DIM - ROPE_DIM,), jnp.float32)]).reshape(1, LANE)

    h = x.reshape(n_tok, d)
    for li in range(depth):
        win, wvat, wwit, wuq, wukvk, wukvvt = _pack_layer(w_in[li], w_uq[li], w_ukv[li])
        (qa, qi, qb, wit, kap, kip, kb, vat, vbt, sga, sgb) = _proj_call(
            h, posc, invl, g_mix[li].reshape(1, d), win, wvat, wwit, g_cq[li].reshape(1, -1),
            wuq, g_ckv[li].reshape(1, -1), wukvk, wukvvt)
        yat, ybt = _attn_call(qa, qi, qb, wit, posr, kap, kip, kb, vat, vbt, posc, batch, seq, topk)
        h = _post_call(h, yat, ybt, sga, sgb, p[li].reshape(n_tok, -1),
                       w_pa[li].astype(bf), w_pb[li].astype(bf), w_out[li].astype(bf),
                       g_mlp[li].reshape(1, d), w_ff1[li].astype(bf), w_ff2[li].astype(bf),
                       g_ple[li].reshape(1, d), w_ple_gate[li].astype(bf), w_ple[li].astype(bf),
                       g_final.reshape(1, d), li == depth - 1)
    return h.reshape(batch, seq, d)
```

```python
import functools

import jax
import jax.numpy as jnp
import numpy as np
from jax import lax
from jax.experimental import pallas as pl
from jax.experimental.pallas import tpu as pltpu

D_MODEL = 1024
N_HEADS_A = 8
N_KV_HEADS_A = 2
HEAD_DIM_A = 64
GROUP_A = N_HEADS_A // N_KV_HEADS_A
N_IDX_HEADS = 4
IDX_DIM = 64
TOPK_MAX = 256
N_HEADS_B = 8
Q_LORA = 384
KV_LORA = 256
NOPE_DIM = 64
ROPE_DIM = 32
HALF_ROPE = ROPE_DIM // 2
V_DIM_B = 64
ROPE_THETA = 10000.0
D_FF = 4 * D_MODEL
PLE_DIM = 256
EPS = 1e-6

LANE = 128
SUBLANE = 8
HEAD_SLOT = 128
VMEM_LIMIT = 56 * 1024 * 1024

TM_PROJ = 512
TM_POST = 512
TQ = 256
KC = 256
FF_CHUNK = 1024
NEG = -1e30
FAR = 1e30
LOG2E = 1.4426950408889634
COUNT_ROWS = 32

C_QA = 0
C_QI = C_QA + N_HEADS_A * HEAD_DIM_A
C_KA = C_QI + N_IDX_HEADS * IDX_DIM
C_SM = C_KA + N_KV_HEADS_A * HEAD_DIM_A
C_CQ = C_SM + LANE
C_CKV = C_CQ + Q_LORA
C_GA = C_CKV + KV_LORA
C_GB = C_GA + D_MODEL
C_END = C_GB + D_MODEL
KA_PAD = 2 * N_KV_HEADS_A * LANE
KI_PAD = 2 * LANE

_NT = (((1,), (1,)), ((), ()))
_TN = (((0,), (0,)), ((), ()))


def _rms(x, g):
    return x * lax.rsqrt(jnp.mean(x * x, axis=-1, keepdims=True) + EPS) * g


def _rope_slot(x, cos_t, sin_dn, sin_up):
    up = pltpu.roll(x, HALF_ROPE, 1)
    dn = pltpu.roll(x, LANE - HALF_ROPE, 1)
    return x * cos_t + dn * sin_dn + up * sin_up


def _proj_kernel(x_ref, pos_ref, invl_ref, gmix_ref, win_ref, wvat_ref, wwit_ref, gcq_ref,
                 wuq_ref, gckv_ref, wukvk_ref, wukvvt_ref,
                 qa_ref, qi_ref, qb_ref, wit_ref, kap_ref, kip_ref, kb_ref, vat_ref, vbt_ref,
                 sga_ref, sgb_ref):
    bf = jnp.bfloat16
    n = _rms(x_ref[...], gmix_ref[...]).astype(bf)

    def cols(c0, c1):
        return jnp.dot(n, win_ref[:, c0:c1], preferred_element_type=jnp.float32)

    cqn = _rms(cols(C_CQ, C_CKV), gcq_ref[...]).astype(bf)
    ckvn = _rms(cols(C_CKV, C_GA), gckv_ref[...]).astype(bf)

    lane = lax.broadcasted_iota(jnp.int32, (1, LANE), 1)
    is_x1 = (lane >= NOPE_DIM) & (lane < NOPE_DIM + HALF_ROPE)
    is_x2 = (lane >= NOPE_DIM + HALF_ROPE) & (lane < NOPE_DIM + ROPE_DIM)
    ang = pos_ref[...] * invl_ref[...]
    cos, sin = jnp.cos(ang), jnp.sin(ang)
    cos_t = jnp.where(is_x1 | is_x2, cos, 1.0)
    sin_dn = jnp.where(is_x1, -sin, 0.0)
    sin_up = jnp.where(is_x2, sin, 0.0)

    qa_ref[...] = (cols(C_QA, C_QI) * (HEAD_DIM_A ** -0.5 * LOG2E)).astype(bf)
    qi_ref[...] = cols(C_QI, C_KA).astype(bf)
    low = lane < LANE // 2

    def half_slabs(z):
        swapped = pltpu.roll(z, LANE // 2, 1)
        return (jnp.where(low, z, 0.0), jnp.where(low, 0.0, swapped),
                jnp.where(low, swapped, 0.0), jnp.where(low, 0.0, z))

    for j, slab in enumerate(half_slabs(cols(C_KA, C_SM))):
        kap_ref[:, j * LANE:(j + 1) * LANE] = slab.astype(bf)
    small = cols(C_SM, C_CQ)
    ki_slabs = half_slabs(small)
    kip_ref[:, :LANE] = ki_slabs[0].astype(bf)
    kip_ref[:, LANE:] = ki_slabs[1].astype(bf)
    sga_ref[...] = jax.nn.sigmoid(cols(C_GA, C_GB)).astype(bf)
    sgb_ref[...] = jax.nn.sigmoid(cols(C_GB, C_END)).astype(bf)
    def put_chunks(ref, vt):
        for j in range(TM_PROJ // KC):
            ref[j] = vt[:, j * KC:(j + 1) * KC].astype(bf)

    put_chunks(vat_ref, lax.dot_general(wvat_ref[...], n, _NT, preferred_element_type=jnp.float32))
    wit_ref[...] = lax.dot_general(wwit_ref[...], n, _NT,
                                   preferred_element_type=jnp.float32) * ((N_IDX_HEADS * IDX_DIM) ** -0.5)

    qscale = (NOPE_DIM + ROPE_DIM) ** -0.5 * LOG2E
    qall = jnp.dot(cqn, wuq_ref[...], preferred_element_type=jnp.float32)
    for h in range(N_HEADS_B):
        sl = slice(h * HEAD_SLOT, (h + 1) * HEAD_SLOT)
        qb_ref[:, sl] = (_rope_slot(qall[:, sl], cos_t, sin_dn, sin_up) * qscale).astype(bf)

    kr =_rope_slot(jnp.where(low, 0.0, small), cos_t, sin_dn, sin_up)
    kall = jnp.dot(ckvn, wukvk_ref[...], preferred_element_type=jnp.float32)
    for h in range(N_HEADS_B):
        sl = slice(h * HEAD_SLOT, (h + 1) * HEAD_SLOT)
        kb_ref[:, sl] = (kall[:, sl] + kr).astype(bf)
    put_chunks(vbt_ref, lax.dot_general(wukvvt_ref[...], ckvn, _NT,
                                        preferred_element_type=jnp.float32))


def _const_spec(shape):
    return pl.BlockSpec(shape, lambda *_: (0,) * len(shape))


def _proj_call(x2, posc, invl, gmix, win, wvat, wwit, gcq, wuq, gckv, wukvk, wukvvt):
    n_tok = x2.shape[0]
    tm = TM_PROJ
    bf = jnp.bfloat16
    row = lambda w: pl.BlockSpec((tm, w), lambda t: (t, 0))
    col = lambda r: pl.BlockSpec((r, tm), lambda t: (0, t))
    chunks = lambda r: pl.BlockSpec((tm // KC, r, KC), lambda t: (t, 0, 0))
    in_specs = [row(D_MODEL), row(1), _const_spec(invl.shape), _const_spec(gmix.shape),
                _const_spec(win.shape), _const_spec(wvat.shape), _const_spec(wwit.shape),
                _const_spec(gcq.shape), _const_spec(wuq.shape), _const_spec(gckv.shape),
                _const_spec(wukvk.shape), _const_spec(wukvvt.shape)]
    out_shape = [
        jax.ShapeDtypeStruct((n_tok, C_QI - C_QA), bf),
        jax.ShapeDtypeStruct((n_tok, C_KA - C_QI), bf),
        jax.ShapeDtypeStruct((n_tok, N_HEADS_B * HEAD_SLOT), bf),
        jax.ShapeDtypeStruct((SUBLANE, n_tok), jnp.float32),
        jax.ShapeDtypeStruct((n_tok, KA_PAD), bf),
        jax.ShapeDtypeStruct((n_tok, KI_PAD), bf),
        jax.ShapeDtypeStruct((n_tok, N_HEADS_B * HEAD_SLOT), bf),
        jax.ShapeDtypeStruct((n_tok // KC, N_KV_HEADS_A * HEAD_DIM_A, KC), bf),
        jax.ShapeDtypeStruct((n_tok // KC, N_HEADS_B * V_DIM_B, KC), bf),
        jax.ShapeDtypeStruct((n_tok, D_MODEL), bf),
        jax.ShapeDtypeStruct((n_tok, D_MODEL), bf),
    ]
    out_specs = [row(C_QI - C_QA), row(C_KA - C_QI), row(N_HEADS_B * HEAD_SLOT), col(SUBLANE),
                 row(KA_PAD), row(KI_PAD), row(N_HEADS_B * HEAD_SLOT),
                 chunks(N_KV_HEADS_A * HEAD_DIM_A), chunks(N_HEADS_B * V_DIM_B),
                 row(D_MODEL), row(D_MODEL)]
    return pl.pallas_call(
        _proj_kernel,
        grid=(n_tok // tm,),
        in_specs=in_specs,
        out_specs=out_specs,
        out_shape=out_shape,
        compiler_params=pltpu.CompilerParams(
            dimension_semantics=("arbitrary",), vmem_limit_bytes=VMEM_LIMIT),
        name="proj",
    )(x2, posc, invl, gmix, win, wvat, wwit, gcq, wuq, gckv, wukvk, wukvvt)


def _col_sum(a):
    r = a.shape[0]
    part = jnp.sum(a.reshape(r // SUBLANE, SUBLANE, a.shape[1]), axis=0)
    return jnp.sum(part, axis=0, keepdims=True)


def _col_max(a):
    r = a.shape[0]
    part = jnp.max(a.reshape(r // SUBLANE, SUBLANE, a.shape[1]), axis=0)
    return jnp.max(part, axis=0, keepdims=True)


def _attn_kernel(qa_ref, qi_ref, qb_ref, wit_ref, posq_ref,
                 kap_ref, kip_ref, kb_ref, vat_ref, vbt_ref, posk_ref,
                 yat_ref, ybt_ref,
                 score_ref, lg_ref, acc_ref, m_ref, mu_ref, l_ref, *, topk):
    bf = jnp.bfloat16
    f32 = jnp.float32
    i = pl.program_id(1)
    nch = i + 1
    row_i = lax.broadcasted_iota(jnp.int32, (KC, TQ), 0)
    col_i = lax.broadcasted_iota(jnp.int32, (KC, TQ), 1)
    causal = row_i <= col_i
    kf = float(topk)

    def kslice(c):
        if isinstance(c, int):
            return pl.ds(c * KC, KC)
        return pl.ds(pl.multiple_of(c * KC, KC), KC)

    w = wit_ref[...]

    def score_chunk(c, carry):
        acc = jnp.zeros((KC, TQ), f32)
        for h in range(N_IDX_HEADS):
            kk = kip_ref[kslice(c), (h % 2) * LANE:(h % 2 + 1) * LANE]
            qq = qi_ref[:, (h // 2) * LANE:(h // 2 + 1) * LANE]
            d = lax.dot_general(kk, qq, _NT, preferred_element_type=f32)
            acc = acc + jnp.maximum(d, 0.0) * w[h:h + 1, :]
        score_ref[c] = acc
        return carry

    lax.fori_loop(0, nch, score_chunk, 0)
    score_ref[i] = jnp.where(causal, score_ref[i], -jnp.inf)

    def count(pred):
        def body(c, acc):
            hit = pred(score_ref[c]).reshape(KC // COUNT_ROWS, COUNT_ROWS, TQ)
            for r in range(KC // COUNT_ROWS):
                acc = jnp.where(hit[r], acc + 1.0, acc)
            return acc
        acc = lax.fori_loop(0, nch, body, jnp.zeros((COUNT_ROWS, TQ), f32))
        return _col_sum(acc)

    def key_to_float(key):
        bits = jnp.where(key >= 0, key, key ^ jnp.int32(0x7FFFFFFF))
        return lax.bitcast_convert_type(bits, f32)

    def search_step(j, key):
        cand = key ^ lax.shift_left(jnp.int32(1), jnp.int32(31) - j)
        thr_c = key_to_float(cand)
        cnt = count(lambda s: s >= thr_c)
        return jnp.where(cnt >= kf, cand, key)

    key0 = jnp.full((1, TQ), jnp.iinfo(jnp.int32).min, jnp.int32)
    key = lax.fori_loop(0, 32, search_step, key0)
    t_glob = i * TQ + lax.broadcasted_iota(jnp.int32, (1, TQ), 1)
    take_all = t_glob < topk
    thr = jnp.where(take_all, jnp.finfo(f32).min, key_to_float(key))
    n_ge = count(lambda s: s >= thr)
    surplus = jnp.where(take_all, 0.0, n_ge - kf)
    posq = posq_ref[...]

    any_surplus = jnp.max(surplus) > 0.0

    def distance(c):
        pk = posk_ref[kslice(c), :]
        return jnp.abs(jnp.concatenate([pk] * (TQ // LANE), axis=1) - posq)

    @pl.when(jnp.logical_not(any_surplus))
    def _select_all_ties():
        def chunk(c, carry):
            score_ref[c] = jnp.where(score_ref[c] >= thr, distance(c), FAR)
            return carry

        lax.fori_loop(0, nch, chunk, 0)

    @pl.when(any_surplus)
    def _select_ranked_ties():
        n_gt = count(lambda s: s > thr)
        need = jnp.where(take_all, float(score_ref.shape[0] * KC + 1), kf - n_gt)
        tri = jnp.where(lax.broadcasted_iota(jnp.int32, (KC, KC), 0)
                        >= lax.broadcasted_iota(jnp.int32, (KC, KC), 1), 1.0, 0.0).astype(bf)

        def chunk(c, taken):
            s = score_ref[c]
            eq = s == thr
            pc = jnp.dot(tri, jnp.where(eq, 1.0, 0.0).astype(bf),
                         preferred_element_type=f32) + taken
            rank = jnp.where(eq, pc, jnp.where(s > thr, -jnp.inf, jnp.inf))
            score_ref[c] = jnp.where(rank <= need, distance(c), FAR)
            return pc[KC - 1:KC, :]

        lax.fori_loop(0, nch, chunk, jnp.zeros((1, TQ), f32))

    def dsa_logits(c, h, first):
        g = h // GROUP_A
        kk = kap_ref[kslice(c), (2 * g + h % 2) * LANE:(2 * g + h % 2 + 1) * LANE]
        qq = qa_ref[:, (h // 2) * LANE:(h // 2 + 1) * LANE]
        slope = 2.0 ** (-8.0 * (h + 1) / N_HEADS_A) * LOG2E
        qk = lax.dot_general(kk, qq, _NT, preferred_element_type=f32)
        return qk - slope * score_ref[c]

    def dsa_values(c, h):
        g = h // GROUP_A
        return vat_ref[c, g * HEAD_DIM_A:(g + 1) * HEAD_DIM_A, :]

    def mla_chunk_of(j):
        return jnp.where(j == 0, i, j - 1)

    def mla_logits(c, h, first):
        kk = kb_ref[kslice(c), h * HEAD_SLOT:(h + 1) * HEAD_SLOT]
        qq = qb_ref[:, h * HEAD_SLOT:(h + 1) * HEAD_SLOT]
        qk = lax.dot_general(kk, qq, _NT, preferred_element_type=f32)
        return jnp.where(causal, qk, NEG) if first else qk

    def mla_values(c, h):
        return vbt_ref[c, h * V_DIM_B:(h + 1) * V_DIM_B, :]

    n_slots = N_HEADS_A + N_HEADS_B
    m_ref[...] = jnp.full(m_ref.shape, NEG, f32)
    mu_ref[...] = jnp.full(mu_ref.shape, NEG, f32)
    l_ref[...] = jnp.zeros(l_ref.shape, f32)
    acc_ref[...] = jnp.zeros(acc_ref.shape, f32)

    def slot_logits(j, s, first):
        if s < N_HEADS_A:
            return dsa_logits(j, s, first)
        return mla_logits(mla_chunk_of(j), s - N_HEADS_A, first)

    def slot_values(j, s):
        if s < N_HEADS_A:
            return dsa_values(j, s)
        return mla_values(mla_chunk_of(j), s - N_HEADS_A)

    def stage(j, s, first):
        lg = slot_logits(j, s, first)
        lg_ref[s] = lg
        m_ref[s:s + 1, :] = jnp.maximum(m_ref[s:s + 1, :], _col_max(lg))

    def consume(j, s):
        m_now = m_ref[s:s + 1, :]
        alpha = jnp.exp2(mu_ref[s:s + 1, :] - m_now)
        mu_ref[s:s + 1, :] = m_now
        p = jnp.exp2(lg_ref[s] - m_now)
        l_ref[s:s + 1, :] = alpha * l_ref[s:s + 1, :] + _col_sum(p)
        pv = jnp.dot(slot_values(j, s), p.astype(bf), preferred_element_type=f32)
        acc_ref[s] = alpha * acc_ref[s] + pv

    for s in range(n_slots):
        stage(0, s, True)

    def sweep(j, carry):
        for s in range(n_slots):
            consume(j, s)
            stage(j + 1, s, False)
        return carry

    lax.fori_loop(0, i, sweep, 0)
    for s in range(n_slots):
        consume(i, s)
        out_ref, h, dv = (yat_ref, s, HEAD_DIM_A) if s < N_HEADS_A else (ybt_ref, s - N_HEADS_A, V_DIM_B)
        inv_l = 1.0 / l_ref[s:s + 1, :]
        out_ref[h * dv:(h + 1) * dv, :] = (acc_ref[s] * inv_l).astype(out_ref.dtype)


def _attn_call(qa, qi, qb, wit, posr, kap, kip, kb, vat, vbt, posc, batch, seq, topk):
    nq = seq // TQ
    bf = jnp.bfloat16
    n_tok = batch * seq
    n_slots = N_HEADS_A + N_HEADS_B
    assert HEAD_DIM_A == V_DIM_B
    qrow = lambda w: pl.BlockSpec((TQ, w), lambda b, i: (b * nq + i, 0))
    qcol = lambda r: pl.BlockSpec((r, TQ), lambda b, i: (0, b * nq + i))
    krow = lambda w: pl.BlockSpec((seq, w), lambda b, i: (b, 0))
    kchunks = lambda r: pl.BlockSpec((seq // KC, r, KC), lambda b, i: (b, 0, 0))
    in_specs = [qrow(qa.shape[1]), qrow(qi.shape[1]), qrow(qb.shape[1]), qcol(SUBLANE), qcol(1),
                krow(kap.shape[1]), krow(kip.shape[1]), krow(kb.shape[1]),
                kchunks(vat.shape[1]), kchunks(vbt.shape[1]), krow(LANE)]
    out_shape = [jax.ShapeDtypeStruct((N_HEADS_A * HEAD_DIM_A, n_tok), bf),
                 jax.ShapeDtypeStruct((N_HEADS_B * V_DIM_B, n_tok), bf)]
    out_specs = [qcol(N_HEADS_A * HEAD_DIM_A), qcol(N_HEADS_B * V_DIM_B)]
    scratch = [pltpu.VMEM((seq // KC, KC, TQ), jnp.float32),
               pltpu.VMEM((n_slots, KC, TQ), jnp.float32),
               pltpu.VMEM((n_slots, HEAD_DIM_A, TQ), jnp.float32),
               pltpu.VMEM((n_slots, TQ), jnp.float32),
               pltpu.VMEM((n_slots, TQ), jnp.float32),
               pltpu.VMEM((n_slots, TQ), jnp.float32)]
    return pl.pallas_call(
        functools.partial(_attn_kernel, topk=topk),
        grid=(batch, nq),
        in_specs=in_specs,
        out_specs=out_specs,
        out_shape=out_shape,
        scratch_shapes=scratch,
        compiler_params=pltpu.CompilerParams(
            dimension_semantics=("arbitrary", "arbitrary"), vmem_limit_bytes=VMEM_LIMIT),
        name="attn",
    )(qa, qi, qb, wit, posr, kap, kip, kb, vat, vbt, posc)


def _post_kernel(x_ref, yat_ref, ybt_ref, sga_ref, sgb_ref, p_ref,
                 wpa_ref, wpb_ref, wout_ref, gmlp_ref, wff1_ref, wff2_ref, gple_ref,
                 wpg_ref, wple_ref, gfin_ref, o_ref, *, final_norm):
    bf = jnp.bfloat16
    f32 = jnp.float32
    a = lax.dot_general(yat_ref[...], wpa_ref[...], _TN, preferred_element_type=f32)
    b = lax.dot_general(ybt_ref[...], wpb_ref[...], _TN, preferred_element_type=f32)
    merged = sga_ref[...].astype(f32) * a + sgb_ref[...].astype(f32) * b
    h = x_ref[...] + jnp.dot(merged.astype(bf), wout_ref[...], preferred_element_type=f32)
    n2 = _rms(h, gmlp_ref[...]).astype(bf)
    ff = jnp.zeros_like(h)
    for c in range(D_FF // FF_CHUNK):
        sl = slice(c * FF_CHUNK, (c + 1) * FF_CHUNK)
        u = jnp.maximum(jnp.dot(n2, wff1_ref[:, sl], preferred_element_type=f32), 0.0)
        ff = ff + jnp.dot((u * u).astype(bf), wff2_ref[sl, :], preferred_element_type=f32)
    h = h + ff
    n3 = _rms(h, gple_ref[...]).astype(bf)
    gate = jax.nn.sigmoid(jnp.dot(n3, wpg_ref[...], preferred_element_type=f32))
    pe = jnp.dot(p_ref[...].astype(bf), wple_ref[...], preferred_element_type=f32)
    h = h + gate * pe
    o_ref[...] = _rms(h, gfin_ref[...]) if final_norm else h


def _post_call(x2, yat, ybt, sga, sgb, p2, wpa, wpb, wout, gmlp, wff1, wff2, gple, wpg, wple, gfin,
               final_norm):
    n_tok = x2.shape[0]
    tm = TM_POST
    row = lambda w: pl.BlockSpec((tm, w), lambda t: (t, 0))
    col = lambda r: pl.BlockSpec((r, tm), lambda t: (0, t))
    wspec = lambda a: pl.BlockSpec(a.shape, lambda t: (0,) * a.ndim, pipeline_mode=pl.Buffered(1))
    in_specs = [row(D_MODEL), col(yat.shape[0]), col(ybt.shape[0]), row(D_MODEL), row(D_MODEL),
                row(PLE_DIM)] + [wspec(a) for a in
                                 (wpa, wpb, wout, gmlp, wff1, wff2, gple, wpg, wple, gfin)]
    return pl.pallas_call(
        functools.partial(_post_kernel, final_norm=final_norm),
        grid=(n_tok // tm,),
        in_specs=in_specs,
        out_specs=row(D_MODEL),
        out_shape=jax.ShapeDtypeStruct((n_tok, D_MODEL), jnp.float32),
        compiler_params=pltpu.CompilerParams(
            dimension_semantics=("arbitrary",), vmem_limit_bytes=VMEM_LIMIT),
        name="post",
    )(x2, yat, ybt, sga, sgb, p2, wpa, wpb, wout, gmlp, wff1, wff2, gple, wpg, wple, gfin)


def _pack_layer(w_in, w_uq, w_ukv):
    bf = jnp.bfloat16
    sizes = (N_HEADS_A * HEAD_DIM_A, N_KV_HEADS_A * HEAD_DIM_A, N_KV_HEADS_A * HEAD_DIM_A,
             N_IDX_HEADS * IDX_DIM, IDX_DIM, N_IDX_HEADS, Q_LORA, KV_LORA, ROPE_DIM,
             D_MODEL, D_MODEL)
    offs = np.concatenate([[0], np.cumsum(sizes)])
    wqa, wka, wva, wqi, wki, wwi, wcq, wckv, wkr, wga, wgb = [
        w_in[:, offs[j]:offs[j + 1]] for j in range(len(sizes))]
    d = w_in.shape[0]
    small_pad = jnp.zeros((d, LANE - IDX_DIM - ROPE_DIM), w_in.dtype)
    win = jnp.concatenate([wqa, wqi, wka, wki, wkr, small_pad, wcq, wckv, wga, wgb],
                          axis=1).astype(bf)
    wvat = wva.T.astype(bf)
    wwit = jnp.concatenate([wwi.T, jnp.zeros((SUBLANE - N_IDX_HEADS, d), w_in.dtype)],
                           axis=0).astype(bf)
    uq = w_uq.reshape(Q_LORA, N_HEADS_B, NOPE_DIM + ROPE_DIM)
    uq = jnp.pad(uq, ((0, 0), (0, 0), (0, HEAD_SLOT - NOPE_DIM - ROPE_DIM)))
    wuq = uq.reshape(Q_LORA, N_HEADS_B * HEAD_SLOT).astype(bf)
    ukv = w_ukv.reshape(KV_LORA, N_HEADS_B, NOPE_DIM + V_DIM_B)
    uk = jnp.pad(ukv[:, :, :NOPE_DIM], ((0, 0), (0, 0), (0, HEAD_SLOT - NOPE_DIM)))
    wukvk = uk.reshape(KV_LORA, N_HEADS_B * HEAD_SLOT).astype(bf)
    wukvvt = ukv[:, :, NOPE_DIM:].reshape(KV_LORA, N_HEADS_B * V_DIM_B).T.astype(bf)
    return win, wvat, wwit, wuq, wukvk, wukvvt


def kernel(x, p, positions, g_mix, w_in, g_cq, w_uq, g_ckv, w_ukv, w_pa, w_pb, w_out, g_mlp,
           w_ff1, w_ff2, g_ple, w_ple_gate, w_ple, g_final):
    bf = jnp.bfloat16
    batch, seq, d = x.shape
    depth = w_in.shape[0]
    n_tok = batch * seq
    topk = min(TOPK_MAX, seq // 4)
    assert d == D_MODEL and seq % TQ == 0 and TQ == KC and topk <= TQ and n_tok % TM_PROJ == 0
    assert IDX_DIM == NOPE_DIM == HEAD_DIM_A == LANE // 2

    posf = positions.astype(jnp.float32)
    posc = posf.reshape(n_tok, 1)
    posr = posf.reshape(1, n_tok)
    inv = ROPE_THETA ** (-jnp.arange(HALF_ROPE, dtype=jnp.float32) / HALF_ROPE)
    invl = jnp.concatenate([jnp.zeros((NOPE_DIM,), jnp.float32), inv, inv,
                            jnp.zeros((HEAD_SLOT - NOPE_DIM - ROPE_DIM,), jnp.float32)]).reshape(1, LANE)

    h = x.reshape(n_tok, d)
    for li in range(depth):
        win, wvat, wwit, wuq, wukvk, wukvvt = _pack_layer(w_in[li], w_uq[li], w_ukv[li])
        (qa, qi, qb, wit, kap, kip, kb, vat, vbt, sga, sgb) = _proj_call(
            h, posc, invl, g_mix[li].reshape(1, d), win, wvat, wwit, g_cq[li].reshape(1, -1),
            wuq, g_ckv[li].reshape(1, -1), wukvk, wukvvt)
        yat, ybt = _attn_call(qa, qi, qb, wit, posr, kap, kip, kb, vat, vbt,
                              jnp.broadcast_to(posc, (n_tok, LANE)), batch, seq, topk)
        h = _post_call(h, yat, ybt, sga, sgb, p[li].reshape(n_tok, -1),
                       w_pa[li].astype(bf), w_pb[li].astype(bf), w_out[li].astype(bf),
                       g_mlp[li].reshape(1, d), w_ff1[li].astype(bf), w_ff2[li].astype(bf),
                       g_ple[li].reshape(1, d), w_ple_gate[li].astype(bf), w_ple[li].astype(bf),
                       g_final.reshape(1, d), li == depth - 1)
    return h.reshape(batch, seq, d)
```

```python
import functools

import jax
import jax.numpy as jnp
import numpy as np
from jax import lax
from jax.experimental import pallas as pl
from jax.experimental.pallas import tpu as pltpu

D_MODEL = 1024
N_HEADS_A = 8
N_KV_HEADS_A = 2
HEAD_DIM_A = 64
GROUP_A = N_HEADS_A // N_KV_HEADS_A
N_IDX_HEADS = 4
IDX_DIM = 64
TOPK_MAX = 256
N_HEADS_B = 8
Q_LORA = 384
KV_LORA = 256
NOPE_DIM = 64
ROPE_DIM = 32
HALF_ROPE = ROPE_DIM // 2
V_DIM_B = 64
ROPE_THETA = 10000.0
D_FF = 4 * D_MODEL
PLE_DIM = 256
EPS = 1e-6

LANE = 128
SUBLANE = 8
HEAD_SLOT = 128
V_ROWS = V_DIM_B + 2 * SUBLANE
VMEM_LIMIT = 56 * 1024 * 1024

TM_PROJ = 512
TM_POST = 512
TQ = 256
KC = 256
FF_CHUNK = 1024
NEG = -1e30
FAR = 1e30
LOG2E = 1.4426950408889634
COUNT_ROWS = 32

C_QA = 0
C_QI = C_QA + N_HEADS_A * HEAD_DIM_A
C_KA = C_QI + N_IDX_HEADS * IDX_DIM
C_SM = C_KA + N_KV_HEADS_A * HEAD_DIM_A
C_CQ = C_SM + LANE
C_CKV = C_CQ + Q_LORA
C_GA = C_CKV + KV_LORA
C_GB = C_GA + D_MODEL
C_END = C_GB + D_MODEL
KA_PAD = 2 * N_KV_HEADS_A * LANE
KI_PAD = 2 * LANE

_NT = (((1,), (1,)), ((), ()))
_TN = (((0,), (0,)), ((), ()))


def _rms(x, g):
    return x * lax.rsqrt(jnp.mean(x * x, axis=-1, keepdims=True) + EPS) * g


def _rope_slot(x, cos_t, sin_dn, sin_up):
    up = pltpu.roll(x, HALF_ROPE, 1)
    dn = pltpu.roll(x, LANE - HALF_ROPE, 1)
    return x * cos_t + dn * sin_dn + up * sin_up


def _proj_kernel(x_ref, pos_ref, invl_ref, gmix_ref, win_ref, wvat_ref, wwit_ref, gcq_ref,
                 wuq_ref, gckv_ref, wukvk_ref, wukvvt_ref,
                 qa_ref, qi_ref, qb_ref, wit_ref, kap_ref, kip_ref, kb_ref, vat_ref, vbt_ref,
                 sga_ref, sgb_ref, posb_ref):
    bf = jnp.bfloat16
    n = _rms(x_ref[...], gmix_ref[...]).astype(bf)
    posb_ref[...] = jnp.broadcast_to(pos_ref[...], posb_ref.shape)

    def cols(c0, c1):
        return jnp.dot(n, win_ref[:, c0:c1], preferred_element_type=jnp.float32)

    cqn = _rms(cols(C_CQ, C_CKV), gcq_ref[...]).astype(bf)
    ckvn = _rms(cols(C_CKV, C_GA), gckv_ref[...]).astype(bf)

    lane = lax.broadcasted_iota(jnp.int32, (1, LANE), 1)
    is_x1 = (lane >= NOPE_DIM) & (lane < NOPE_DIM + HALF_ROPE)
    is_x2 = (lane >= NOPE_DIM + HALF_ROPE) & (lane < NOPE_DIM + ROPE_DIM)
    ang = pos_ref[...] * invl_ref[...]
    cos, sin = jnp.cos(ang), jnp.sin(ang)
    cos_t = jnp.where(is_x1 | is_x2, cos, 1.0)
    sin_dn = jnp.where(is_x1, -sin, 0.0)
    sin_up = jnp.where(is_x2, sin, 0.0)

    qa_ref[...] = (cols(C_QA, C_QI) * (HEAD_DIM_A ** -0.5 * LOG2E)).astype(bf)
    qi_ref[...] = cols(C_QI, C_KA).astype(bf)
    low = lane < LANE // 2

    def half_slabs(z):
        swapped = pltpu.roll(z, LANE // 2, 1)
        return (jnp.where(low, z, 0.0), jnp.where(low, 0.0, swapped),
                jnp.where(low, swapped, 0.0), jnp.where(low, 0.0, z))

    for j, slab in enumerate(half_slabs(cols(C_KA, C_SM))):
        kap_ref[:, j * LANE:(j + 1) * LANE] = slab.astype(bf)
    small = cols(C_SM, C_CQ)
    ki_slabs = half_slabs(small)
    kip_ref[:, :LANE] = ki_slabs[0].astype(bf)
    kip_ref[:, LANE:] = ki_slabs[1].astype(bf)
    sga_ref[...] = jax.nn.sigmoid(cols(C_GA, C_GB)).astype(bf)
    sgb_ref[...] = jax.nn.sigmoid(cols(C_GB, C_END)).astype(bf)
    def put_chunks(ref, vt):
        ones = jnp.ones((V_ROWS - V_DIM_B, KC), bf)
        for j in range(TM_PROJ // KC):
            for g in range(vt.shape[0] // V_DIM_B):
                ref[j, g * V_ROWS:g * V_ROWS + V_DIM_B, :] = (
                    vt[g * V_DIM_B:(g + 1) * V_DIM_B, j * KC:(j + 1) * KC].astype(bf))
                ref[j, g * V_ROWS + V_DIM_B:(g + 1) * V_ROWS, :] = ones

    put_chunks(vat_ref, lax.dot_general(wvat_ref[...], n, _NT, preferred_element_type=jnp.float32))
    wit_ref[...] = lax.dot_general(wwit_ref[...], n, _NT,
                                   preferred_element_type=jnp.float32) * ((N_IDX_HEADS * IDX_DIM) ** -0.5)

    qscale = (NOPE_DIM + ROPE_DIM) ** -0.5 * LOG2E
    qall = jnp.dot(cqn, wuq_ref[...], preferred_element_type=jnp.float32)
    for h in range(N_HEADS_B):
        sl = slice(h * HEAD_SLOT, (h + 1) * HEAD_SLOT)
        qb_ref[:, sl] = (_rope_slot(qall[:, sl], cos_t, sin_dn, sin_up) * qscale).astype(bf)

    kr =_rope_slot(jnp.where(low, 0.0, small), cos_t, sin_dn, sin_up)
    kall = jnp.dot(ckvn, wukvk_ref[...], preferred_element_type=jnp.float32)
    for h in range(N_HEADS_B):
        sl = slice(h * HEAD_SLOT, (h + 1) * HEAD_SLOT)
        kb_ref[:, sl] = (kall[:, sl] + kr).astype(bf)
    put_chunks(vbt_ref, lax.dot_general(wukvvt_ref[...], ckvn, _NT,
                                        preferred_element_type=jnp.float32))


def _const_spec(shape):
    return pl.BlockSpec(shape, lambda *_: (0,) * len(shape))


def _proj_call(x2, posc, invl, gmix, win, wvat, wwit, gcq, wuq, gckv, wukvk, wukvvt):
    n_tok = x2.shape[0]
    tm = TM_PROJ
    bf = jnp.bfloat16
    row = lambda w: pl.BlockSpec((tm, w), lambda t: (t, 0))
    col = lambda r: pl.BlockSpec((r, tm), lambda t: (0, t))
    chunks = lambda r: pl.BlockSpec((tm // KC, r, KC), lambda t: (t, 0, 0))
    in_specs = [row(D_MODEL), row(1), _const_spec(invl.shape), _const_spec(gmix.shape),
                _const_spec(win.shape), _const_spec(wvat.shape), _const_spec(wwit.shape),
                _const_spec(gcq.shape), _const_spec(wuq.shape), _const_spec(gckv.shape),
                _const_spec(wukvk.shape), _const_spec(wukvvt.shape)]
    out_shape = [
        jax.ShapeDtypeStruct((n_tok, C_QI - C_QA), bf),
        jax.ShapeDtypeStruct((n_tok, C_KA - C_QI), bf),
        jax.ShapeDtypeStruct((n_tok, N_HEADS_B * HEAD_SLOT), bf),
        jax.ShapeDtypeStruct((SUBLANE, n_tok), jnp.float32),
        jax.ShapeDtypeStruct((n_tok, KA_PAD), bf),
        jax.ShapeDtypeStruct((n_tok, KI_PAD), bf),
        jax.ShapeDtypeStruct((n_tok, N_HEADS_B * HEAD_SLOT), bf),
        jax.ShapeDtypeStruct((n_tok // KC, N_KV_HEADS_A * V_ROWS, KC), bf),
        jax.ShapeDtypeStruct((n_tok // KC, N_HEADS_B * V_ROWS, KC), bf),
        jax.ShapeDtypeStruct((n_tok, D_MODEL), bf),
        jax.ShapeDtypeStruct((n_tok, D_MODEL), bf),
        jax.ShapeDtypeStruct((n_tok, LANE), jnp.float32),
    ]
    out_specs = [row(C_QI - C_QA), row(C_KA - C_QI), row(N_HEADS_B * HEAD_SLOT), col(SUBLANE),
                 row(KA_PAD), row(KI_PAD), row(N_HEADS_B * HEAD_SLOT),
                 chunks(N_KV_HEADS_A * V_ROWS), chunks(N_HEADS_B * V_ROWS),
                 row(D_MODEL), row(D_MODEL), row(LANE)]
    return pl.pallas_call(
        _proj_kernel,
        grid=(n_tok // tm,),
        in_specs=in_specs,
        out_specs=out_specs,
        out_shape=out_shape,
        compiler_params=pltpu.CompilerParams(
            dimension_semantics=("arbitrary",), vmem_limit_bytes=VMEM_LIMIT),
        name="proj",
    )(x2, posc, invl, gmix, win, wvat, wwit, gcq, wuq, gckv, wukvk, wukvvt)


def _col_sum(a):
    r = a.shape[0]
    part = jnp.sum(a.reshape(r // SUBLANE, SUBLANE, a.shape[1]), axis=0)
    return jnp.sum(part, axis=0, keepdims=True)


def _col_max(a):
    r = a.shape[0]
    part = jnp.max(a.reshape(r // SUBLANE, SUBLANE, a.shape[1]), axis=0)
    return jnp.max(part, axis=0, keepdims=True)


def _attn_kernel(qa_ref, qi_ref, qb_ref, wit_ref, posq_ref,
                 kap_ref, kip_ref, kb_ref, vat_ref, vbt_ref, posk_ref,
                 yat_ref, ybt_ref,
                 score_ref, lg_ref, acc_ref, m_ref, mu_ref, *, topk):
    bf = jnp.bfloat16
    f32 = jnp.float32
    i = pl.program_id(1)
    nch = i + 1
    row_i = lax.broadcasted_iota(jnp.int32, (KC, TQ), 0)
    col_i = lax.broadcasted_iota(jnp.int32, (KC, TQ), 1)
    causal = row_i <= col_i
    kf = float(topk)

    def kslice(c):
        if isinstance(c, int):
            return pl.ds(c * KC, KC)
        return pl.ds(pl.multiple_of(c * KC, KC), KC)

    w = wit_ref[...]

    def score_chunk(c, carry):
        acc = jnp.zeros((KC, TQ), f32)
        for h in range(N_IDX_HEADS):
            kk = kip_ref[kslice(c), (h % 2) * LANE:(h % 2 + 1) * LANE]
            qq = qi_ref[:, (h // 2) * LANE:(h // 2 + 1) * LANE]
            d = lax.dot_general(kk, qq, _NT, preferred_element_type=f32)
            acc = acc + jnp.maximum(d, 0.0) * w[h:h + 1, :]
        score_ref[c] = acc
        return carry

    lax.fori_loop(0, nch, score_chunk, 0)
    score_ref[i] = jnp.where(causal, score_ref[i], -jnp.inf)

    def count(pred):
        def body(c, acc):
            hit = pred(score_ref[c]).reshape(KC // COUNT_ROWS, COUNT_ROWS, TQ)
            for r in range(KC // COUNT_ROWS):
                acc = jnp.where(hit[r], acc + 1.0, acc)
            return acc
        acc = lax.fori_loop(0, nch, body, jnp.zeros((COUNT_ROWS, TQ), f32))
        return _col_sum(acc)

    def key_to_float(key):
        bits = jnp.where(key >= 0, key, key ^ jnp.int32(0x7FFFFFFF))
        return lax.bitcast_convert_type(bits, f32)

    def search_step(j, key):
        cand = key ^ lax.shift_left(jnp.int32(1), jnp.int32(31) - j)
        thr_c = key_to_float(cand)
        cnt = count(lambda s: s >= thr_c)
        return jnp.where(cnt >= kf, cand, key)

    key0 = jnp.full((1, TQ), jnp.iinfo(jnp.int32).min, jnp.int32)
    key = lax.fori_loop(0, 32, search_step, key0)
    t_glob = i * TQ + lax.broadcasted_iota(jnp.int32, (1, TQ), 1)
    take_all = t_glob < topk
    thr = jnp.where(take_all, jnp.finfo(f32).min, key_to_float(key))
    n_ge = count(lambda s: s >= thr)
    surplus = jnp.where(take_all, 0.0, n_ge - kf)
    posq = posq_ref[...]

    any_surplus = jnp.max(surplus) > 0.0

    def distance(c):
        pk = posk_ref[kslice(c), :]
        return jnp.abs(jnp.concatenate([pk] * (TQ // LANE), axis=1) - posq)

    @pl.when(jnp.logical_not(any_surplus))
    def _select_all_ties():
        def chunk(c, carry):
            score_ref[c] = jnp.where(score_ref[c] >= thr, distance(c), FAR)
            return carry

        lax.fori_loop(0, nch, chunk, 0)

    @pl.when(any_surplus)
    def _select_ranked_ties():
        n_gt = count(lambda s: s > thr)
        need = jnp.where(take_all, float(score_ref.shape[0] * KC + 1), kf - n_gt)
        tri = jnp.where(lax.broadcasted_iota(jnp.int32, (KC, KC), 0)
                        >= lax.broadcasted_iota(jnp.int32, (KC, KC), 1), 1.0, 0.0).astype(bf)

        def chunk(c, taken):
            s = score_ref[c]
            eq = s == thr
            pc = jnp.dot(tri, jnp.where(eq, 1.0, 0.0).astype(bf),
                         preferred_element_type=f32) + taken
            rank = jnp.where(eq, pc, jnp.where(s > thr, -jnp.inf, jnp.inf))
            score_ref[c] = jnp.where(rank <= need, distance(c), FAR)
            return pc[KC - 1:KC, :]

        lax.fori_loop(0, nch, chunk, jnp.zeros((1, TQ), f32))

    def dsa_logits(c, h, first):
        g = h // GROUP_A
        kk = kap_ref[kslice(c), (2 * g + h % 2) * LANE:(2 * g + h % 2 + 1) * LANE]
        qq = qa_ref[:, (h // 2) * LANE:(h // 2 + 1) * LANE]
        slope = 2.0 ** (-8.0 * (h + 1) / N_HEADS_A) * LOG2E
        qk = lax.dot_general(kk, qq, _NT, preferred_element_type=f32)
        return qk - slope * score_ref[c]

    def dsa_values(c, h):
        g = h // GROUP_A
        return vat_ref[c, g * V_ROWS:(g + 1) * V_ROWS, :]

    def mla_chunk_of(j):
        return jnp.where(j == 0, i, j - 1)

    def mla_logits(c, h, first):
        kk = kb_ref[kslice(c), h * HEAD_SLOT:(h + 1) * HEAD_SLOT]
        qq = qb_ref[:, h * HEAD_SLOT:(h + 1) * HEAD_SLOT]
        qk = lax.dot_general(kk, qq, _NT, preferred_element_type=f32)
        return jnp.where(causal, qk, NEG) if first else qk

    def mla_values(c, h):
        return vbt_ref[c, h * V_ROWS:(h + 1) * V_ROWS, :]

    n_slots = N_HEADS_A + N_HEADS_B
    m_ref[...] = jnp.full(m_ref.shape, NEG, f32)
    mu_ref[...] = jnp.full(mu_ref.shape, NEG, f32)
    acc_ref[...] = jnp.zeros(acc_ref.shape, f32)

    def slot_logits(j, s, first):
        if s < N_HEADS_A:
            return dsa_logits(j, s, first)
        return mla_logits(mla_chunk_of(j), s - N_HEADS_A, first)

    def slot_values(j, s):
        if s < N_HEADS_A:
            return dsa_values(j, s)
        return mla_values(mla_chunk_of(j), s - N_HEADS_A)

    def stage(j, s, first):
        lg = slot_logits(j, s, first)
        lg_ref[s] = lg
        m_ref[s:s + 1, :] = jnp.maximum(m_ref[s:s + 1, :], _col_max(lg))

    def consume(j, s):
        m_now = m_ref[s:s + 1, :]
        alpha = jnp.exp2(mu_ref[s:s + 1, :] - m_now)
        mu_ref[s:s + 1, :] = m_now
        p = jnp.exp2(lg_ref[s] - m_now)
        pv = jnp.dot(slot_values(j, s), p.astype(bf), preferred_element_type=f32)
        acc_ref[s] = alpha * acc_ref[s] + pv

    for s in range(n_slots):
        stage(0, s, True)

    def sweep(j, carry):
        for s in range(n_slots):
            consume(j, s)
            stage(j + 1, s, False)
        return carry

    lax.fori_loop(0, i, sweep, 0)
    for s in range(n_slots):
        consume(i, s)
        out_ref, h, dv = (yat_ref, s, HEAD_DIM_A) if s < N_HEADS_A else (ybt_ref, s - N_HEADS_A, V_DIM_B)
        inv_l = 1.0 / acc_ref[s, dv:dv + 1, :]
        out_ref[h * dv:(h + 1) * dv, :] = (acc_ref[s, :dv, :] * inv_l).astype(out_ref.dtype)


def _attn_call(qa, qi, qb, wit, posr, kap, kip, kb, vat, vbt, posc, batch, seq, topk):
    nq = seq // TQ
    bf = jnp.bfloat16
    n_tok = batch * seq
    n_slots = N_HEADS_A + N_HEADS_B
    assert HEAD_DIM_A == V_DIM_B
    qrow = lambda w: pl.BlockSpec((TQ, w), lambda b, i: (b * nq + i, 0))
    qcol = lambda r: pl.BlockSpec((r, TQ), lambda b, i: (0, b * nq + i))
    krow = lambda w: pl.BlockSpec((seq, w), lambda b, i: (b, 0))
    kchunks = lambda r: pl.BlockSpec((seq // KC, r, KC), lambda b, i: (b, 0, 0))
    in_specs = [qrow(qa.shape[1]), qrow(qi.shape[1]), qrow(qb.shape[1]), qcol(SUBLANE), qcol(1),
                krow(kap.shape[1]), krow(kip.shape[1]), krow(kb.shape[1]),
                kchunks(vat.shape[1]), kchunks(vbt.shape[1]), krow(LANE)]
    out_shape = [jax.ShapeDtypeStruct((N_HEADS_A * HEAD_DIM_A, n_tok), bf),
                 jax.ShapeDtypeStruct((N_HEADS_B * V_DIM_B, n_tok), bf)]
    out_specs = [qcol(N_HEADS_A * HEAD_DIM_A), qcol(N_HEADS_B * V_DIM_B)]
    scratch = [pltpu.VMEM((seq // KC, KC, TQ), jnp.float32),
               pltpu.VMEM((n_slots, KC, TQ), jnp.float32),
               pltpu.VMEM((n_slots, V_ROWS, TQ), jnp.float32),
               pltpu.VMEM((n_slots, TQ), jnp.float32),
               pltpu.VMEM((n_slots, TQ), jnp.float32)]
    return pl.pallas_call(
        functools.partial(_attn_kernel, topk=topk),
        grid=(batch, nq),
        in_specs=in_specs,
        out_specs=out_specs,
        out_shape=out_shape,
        scratch_shapes=scratch,
        compiler_params=pltpu.CompilerParams(
            dimension_semantics=("arbitrary", "arbitrary"), vmem_limit_bytes=VMEM_LIMIT),
        name="attn",
    )(qa, qi, qb, wit, posr, kap, kip, kb, vat, vbt, posc)


def _post_kernel(x_ref, yat_ref, ybt_ref, sga_ref, sgb_ref, p_ref,
                 wpa_ref, wpb_ref, wout_ref, gmlp_ref, wff1_ref, wff2_ref, gple_ref,
                 wpg_ref, wple_ref, gfin_ref, o_ref, *, final_norm):
    bf = jnp.bfloat16
    f32 = jnp.float32
    a = lax.dot_general(yat_ref[...], wpa_ref[...], _TN, preferred_element_type=f32)
    b = lax.dot_general(ybt_ref[...], wpb_ref[...], _TN, preferred_element_type=f32)
    merged = sga_ref[...].astype(f32) * a + sgb_ref[...].astype(f32) * b
    h = x_ref[...] + jnp.dot(merged.astype(bf), wout_ref[...], preferred_element_type=f32)
    n2 = _rms(h, gmlp_ref[...]).astype(bf)
    ff = jnp.zeros_like(h)
    for c in range(D_FF // FF_CHUNK):
        sl = slice(c * FF_CHUNK, (c + 1) * FF_CHUNK)
        u = jnp.maximum(jnp.dot(n2, wff1_ref[:, sl], preferred_element_type=f32), 0.0)
        ff = ff + jnp.dot((u * u).astype(bf), wff2_ref[sl, :], preferred_element_type=f32)
    h = h + ff
    n3 = _rms(h, gple_ref[...]).astype(bf)
    gate = jax.nn.sigmoid(jnp.dot(n3, wpg_ref[...], preferred_element_type=f32))
    pe = jnp.dot(p_ref[...].astype(bf), wple_ref[...], preferred_element_type=f32)
    h = h + gate * pe
    o_ref[...] = _rms(h, gfin_ref[...]) if final_norm else h


def _post_call(x2, yat, ybt, sga, sgb, p2, wpa, wpb, wout, gmlp, wff1, wff2, gple, wpg, wple, gfin,
               final_norm):
    n_tok = x2.shape[0]
    tm = TM_POST
    row = lambda w: pl.BlockSpec((tm, w), lambda t: (t, 0))
    col = lambda r: pl.BlockSpec((r, tm), lambda t: (0, t))
    wspec = lambda a: pl.BlockSpec(a.shape, lambda t: (0,) * a.ndim, pipeline_mode=pl.Buffered(1))
    in_specs = [row(D_MODEL), col(yat.shape[0]), col(ybt.shape[0]), row(D_MODEL), row(D_MODEL),
                row(PLE_DIM)] + [wspec(a) for a in
                                 (wpa, wpb, wout, gmlp, wff1, wff2, gple, wpg, wple, gfin)]
    return pl.pallas_call(
        functools.partial(_post_kernel, final_norm=final_norm),
        grid=(n_tok // tm,),
        in_specs=in_specs,
        out_specs=row(D_MODEL),
        out_shape=jax.ShapeDtypeStruct((n_tok, D_MODEL), jnp.float32),
        compiler_params=pltpu.CompilerParams(
            dimension_semantics=("arbitrary",), vmem_limit_bytes=VMEM_LIMIT),
        name="post",
    )(x2, yat, ybt, sga, sgb, p2, wpa, wpb, wout, gmlp, wff1, wff2, gple, wpg, wple, gfin)


def _pack_layer(w_in, w_uq, w_ukv):
    bf = jnp.bfloat16
    sizes = (N_HEADS_A * HEAD_DIM_A, N_KV_HEADS_A * HEAD_DIM_A, N_KV_HEADS_A * HEAD_DIM_A,
             N_IDX_HEADS * IDX_DIM, IDX_DIM, N_IDX_HEADS, Q_LORA, KV_LORA, ROPE_DIM,
             D_MODEL, D_MODEL)
    offs = np.concatenate([[0], np.cumsum(sizes)])
    wqa, wka, wva, wqi, wki, wwi, wcq, wckv, wkr, wga, wgb = [
        w_in[:, offs[j]:offs[j + 1]] for j in range(len(sizes))]
    d = w_in.shape[0]
    small_pad = jnp.zeros((d, LANE - IDX_DIM - ROPE_DIM), w_in.dtype)
    win = jnp.concatenate([wqa, wqi, wka, wki, wkr, small_pad, wcq, wckv, wga, wgb],
                          axis=1).astype(bf)
    wvat = wva.T.astype(bf)
    wwit = jnp.concatenate([wwi.T, jnp.zeros((SUBLANE - N_IDX_HEADS, d), w_in.dtype)],
                           axis=0).astype(bf)
    uq = w_uq.reshape(Q_LORA, N_HEADS_B, NOPE_DIM + ROPE_DIM)
    uq = jnp.pad(uq, ((0, 0), (0, 0), (0, HEAD_SLOT - NOPE_DIM - ROPE_DIM)))
    wuq = uq.reshape(Q_LORA, N_HEADS_B * HEAD_SLOT).astype(bf)
    ukv = w_ukv.reshape(KV_LORA, N_HEADS_B, NOPE_DIM + V_DIM_B)
    uk = jnp.pad(ukv[:, :, :NOPE_DIM], ((0, 0), (0, 0), (0, HEAD_SLOT - NOPE_DIM)))
    wukvk = uk.reshape(KV_LORA, N_HEADS_B * HEAD_SLOT).astype(bf)
    wukvvt = ukv[:, :, NOPE_DIM:].reshape(KV_LORA, N_HEADS_B * V_DIM_B).T.astype(bf)
    return win, wvat, wwit, wuq, wukvk, wukvvt


def kernel(x, p, positions, g_mix, w_in, g_cq, w_uq, g_ckv, w_ukv, w_pa, w_pb, w_out, g_mlp,
           w_ff1, w_ff2, g_ple, w_ple_gate, w_ple, g_final):
    bf = jnp.bfloat16
    batch, seq, d = x.shape
    depth = w_in.shape[0]
    n_tok = batch * seq
    topk = min(TOPK_MAX, seq // 4)
    assert d == D_MODEL and seq % TQ == 0 and TQ == KC and topk <= TQ and n_tok % TM_PROJ == 0
    assert IDX_DIM == NOPE_DIM == HEAD_DIM_A == LANE // 2

    posf = positions.astype(jnp.float32)
    posc = posf.reshape(n_tok, 1)
    posr = posf.reshape(1, n_tok)
    inv = ROPE_THETA ** (-jnp.arange(HALF_ROPE, dtype=jnp.float32) / HALF_ROPE)
    invl = jnp.concatenate([jnp.zeros((NOPE_DIM,), jnp.float32), inv, inv,
                            jnp.zeros((HEAD_SLOT - NOPE_DIM - ROPE_DIM,), jnp.float32)]).reshape(1, LANE)

    h = x.reshape(n_tok, d)
    for li in range(depth):
        win, wvat, wwit, wuq, wukvk, wukvvt = _pack_layer(w_in[li], w_uq[li], w_ukv[li])
        (qa, qi, qb, wit, kap, kip, kb, vat, vbt, sga, sgb, posb) = _proj_call(
            h, posc, invl, g_mix[li].reshape(1, d), win, wvat, wwit, g_cq[li].reshape(1, -1),
            wuq, g_ckv[li].reshape(1, -1), wukvk, wukvvt)
        yat, ybt = _attn_call(qa, qi, qb, wit, posr, kap, kip, kb, vat, vbt, posb, batch, seq, topk)
        h = _post_call(h, yat, ybt, sga, sgb, p[li].reshape(n_tok, -1),
                       w_pa[li].astype(bf), w_pb[li].astype(bf), w_out[li].astype(bf),
                       g_mlp[li].reshape(1, d), w_ff1[li].astype(bf), w_ff2[li].astype(bf),
                       g_ple[li].reshape(1, d), w_ple_gate[li].astype(bf), w_ple[li].astype(bf),
                       g_final.reshape(1, d), li == depth - 1)
    return h.reshape(batch, seq, d)
```

```python
import functools

import jax
import jax.numpy as jnp
import numpy as np
from jax import lax
from jax.experimental import pallas as pl
from jax.experimental.pallas import tpu as pltpu

D_MODEL = 1024
N_HEADS_A = 8
N_KV_HEADS_A = 2
HEAD_DIM_A = 64
GROUP_A = N_HEADS_A // N_KV_HEADS_A
N_IDX_HEADS = 4
IDX_DIM = 64
TOPK_MAX = 256
N_HEADS_B = 8
Q_LORA = 384
KV_LORA = 256
NOPE_DIM = 64
ROPE_DIM = 32
HALF_ROPE = ROPE_DIM // 2
V_DIM_B = 64
ROPE_THETA = 10000.0
D_FF = 4 * D_MODEL
PLE_DIM = 256
EPS = 1e-6

LANE = 128
SUBLANE = 8
HEAD_SLOT = 128
V_ROWS = V_DIM_B + 2 * SUBLANE
VMEM_LIMIT = 56 * 1024 * 1024

TM_PROJ = 512
TM_POST = 512
TQ = 256
KC = 256
FF_CHUNK = 1024
NEG = -1e30
FAR = 1e30
LOG2E = 1.4426950408889634
COUNT_ROWS = 32
FINE_BITS = 18

C_QA = 0
C_QI = C_QA + N_HEADS_A * HEAD_DIM_A
C_KA = C_QI + N_IDX_HEADS * IDX_DIM
C_SM = C_KA + N_KV_HEADS_A * HEAD_DIM_A
C_CQ = C_SM + LANE
C_CKV = C_CQ + Q_LORA
C_GA = C_CKV + KV_LORA
C_GB = C_GA + D_MODEL
C_END = C_GB + D_MODEL
KA_PAD = 2 * N_KV_HEADS_A * LANE
KI_PAD = 2 * LANE

_NT = (((1,), (1,)), ((), ()))
_TN = (((0,), (0,)), ((), ()))


def _rms(x, g):
    return x * lax.rsqrt(jnp.mean(x * x, axis=-1, keepdims=True) + EPS) * g


def _rope_slot(x, cos_t, sin_dn, sin_up):
    up = pltpu.roll(x, HALF_ROPE, 1)
    dn = pltpu.roll(x, LANE - HALF_ROPE, 1)
    return x * cos_t + dn * sin_dn + up * sin_up


def _proj_kernel(x_ref, pos_ref, invl_ref, gmix_ref, win_ref, wvat_ref, wwit_ref, gcq_ref,
                 wuq_ref, gckv_ref, wukvk_ref, wukvvt_ref,
                 qa_ref, qi_ref, qb_ref, wit_ref, kap_ref, kip_ref, kb_ref, vat_ref, vbt_ref,
                 sga_ref, sgb_ref, posb_ref):
    bf = jnp.bfloat16
    n = _rms(x_ref[...], gmix_ref[...]).astype(bf)
    posb_ref[...] = jnp.broadcast_to(pos_ref[...], posb_ref.shape)

    def cols(c0, c1):
        return jnp.dot(n, win_ref[:, c0:c1], preferred_element_type=jnp.float32)

    cqn = _rms(cols(C_CQ, C_CKV), gcq_ref[...]).astype(bf)
    ckvn = _rms(cols(C_CKV, C_GA), gckv_ref[...]).astype(bf)

    lane = lax.broadcasted_iota(jnp.int32, (1, LANE), 1)
    is_x1 = (lane >= NOPE_DIM) & (lane < NOPE_DIM + HALF_ROPE)
    is_x2 = (lane >= NOPE_DIM + HALF_ROPE) & (lane < NOPE_DIM + ROPE_DIM)
    ang = pos_ref[...] * invl_ref[...]
    cos, sin = jnp.cos(ang), jnp.sin(ang)
    cos_t = jnp.where(is_x1 | is_x2, cos, 1.0)
    sin_dn = jnp.where(is_x1, -sin, 0.0)
    sin_up = jnp.where(is_x2, sin, 0.0)

    qa_ref[...] = (cols(C_QA, C_QI) * (HEAD_DIM_A ** -0.5 * LOG2E)).astype(bf)
    qi_ref[...] = cols(C_QI, C_KA).astype(bf)
    low = lane < LANE // 2

    def half_slabs(z):
        swapped = pltpu.roll(z, LANE // 2, 1)
        return (jnp.where(low, z, 0.0), jnp.where(low, 0.0, swapped),
                jnp.where(low, swapped, 0.0), jnp.where(low, 0.0, z))

    for j, slab in enumerate(half_slabs(cols(C_KA, C_SM))):
        kap_ref[:, j * LANE:(j + 1) * LANE] = slab.astype(bf)
    small = cols(C_SM, C_CQ)
    ki_slabs = half_slabs(small)
    kip_ref[:, :LANE] = ki_slabs[0].astype(bf)
    kip_ref[:, LANE:] = ki_slabs[1].astype(bf)
    sga_ref[...] = jax.nn.sigmoid(cols(C_GA, C_GB)).astype(bf)
    sgb_ref[...] = jax.nn.sigmoid(cols(C_GB, C_END)).astype(bf)
    def put_chunks(ref, vt):
        ones = jnp.ones((V_ROWS - V_DIM_B, KC), bf)
        for j in range(TM_PROJ // KC):
            for g in range(vt.shape[0] // V_DIM_B):
                ref[j, g * V_ROWS:g * V_ROWS + V_DIM_B, :] = (
                    vt[g * V_DIM_B:(g + 1) * V_DIM_B, j * KC:(j + 1) * KC].astype(bf))
                ref[j, g * V_ROWS + V_DIM_B:(g + 1) * V_ROWS, :] = ones

    put_chunks(vat_ref, lax.dot_general(wvat_ref[...], n, _NT, preferred_element_type=jnp.float32))
    wit_ref[...] = lax.dot_general(wwit_ref[...], n, _NT,
                                   preferred_element_type=jnp.float32) * ((N_IDX_HEADS * IDX_DIM) ** -0.5)

    qscale = (NOPE_DIM + ROPE_DIM) ** -0.5 * LOG2E
    qall = jnp.dot(cqn, wuq_ref[...], preferred_element_type=jnp.float32)
    for h in range(N_HEADS_B):
        sl = slice(h * HEAD_SLOT, (h + 1) * HEAD_SLOT)
        qb_ref[:, sl] = (_rope_slot(qall[:, sl], cos_t, sin_dn, sin_up) * qscale).astype(bf)

    kr =_rope_slot(jnp.where(low, 0.0, small), cos_t, sin_dn, sin_up)
    kall = jnp.dot(ckvn, wukvk_ref[...], preferred_element_type=jnp.float32)
    for h in range(N_HEADS_B):
        sl = slice(h * HEAD_SLOT, (h + 1) * HEAD_SLOT)
        kb_ref[:, sl] = (kall[:, sl] + kr).astype(bf)
    put_chunks(vbt_ref, lax.dot_general(wukvvt_ref[...], ckvn, _NT,
                                        preferred_element_type=jnp.float32))


def _const_spec(shape):
    return pl.BlockSpec(shape, lambda *_: (0,) * len(shape))


def _proj_call(x2, posc, invl, gmix, win, wvat, wwit, gcq, wuq, gckv, wukvk, wukvvt):
    n_tok = x2.shape[0]
    tm = TM_PROJ
    bf = jnp.bfloat16
    row = lambda w: pl.BlockSpec((tm, w), lambda t: (t, 0))
    col = lambda r: pl.BlockSpec((r, tm), lambda t: (0, t))
    chunks = lambda r: pl.BlockSpec((tm // KC, r, KC), lambda t: (t, 0, 0))
    in_specs = [row(D_MODEL), row(1), _const_spec(invl.shape), _const_spec(gmix.shape),
                _const_spec(win.shape), _const_spec(wvat.shape), _const_spec(wwit.shape),
                _const_spec(gcq.shape), _const_spec(wuq.shape), _const_spec(gckv.shape),
                _const_spec(wukvk.shape), _const_spec(wukvvt.shape)]
    out_shape = [
        jax.ShapeDtypeStruct((n_tok, C_QI - C_QA), bf),
        jax.ShapeDtypeStruct((n_tok, C_KA - C_QI), bf),
        jax.ShapeDtypeStruct((n_tok, N_HEADS_B * HEAD_SLOT), bf),
        jax.ShapeDtypeStruct((SUBLANE, n_tok), jnp.float32),
        jax.ShapeDtypeStruct((n_tok, KA_PAD), bf),
        jax.ShapeDtypeStruct((n_tok, KI_PAD), bf),
        jax.ShapeDtypeStruct((n_tok, N_HEADS_B * HEAD_SLOT), bf),
        jax.ShapeDtypeStruct((n_tok // KC, N_KV_HEADS_A * V_ROWS, KC), bf),
        jax.ShapeDtypeStruct((n_tok // KC, N_HEADS_B * V_ROWS, KC), bf),
        jax.ShapeDtypeStruct((n_tok, D_MODEL), bf),
        jax.ShapeDtypeStruct((n_tok, D_MODEL), bf),
        jax.ShapeDtypeStruct((n_tok, LANE), jnp.float32),
    ]
    out_specs = [row(C_QI - C_QA), row(C_KA - C_QI), row(N_HEADS_B * HEAD_SLOT), col(SUBLANE),
                 row(KA_PAD), row(KI_PAD), row(N_HEADS_B * HEAD_SLOT),
                 chunks(N_KV_HEADS_A * V_ROWS), chunks(N_HEADS_B * V_ROWS),
                 row(D_MODEL), row(D_MODEL), row(LANE)]
    return pl.pallas_call(
        _proj_kernel,
        grid=(n_tok // tm,),
        in_specs=in_specs,
        out_specs=out_specs,
        out_shape=out_shape,
        compiler_params=pltpu.CompilerParams(
            dimension_semantics=("arbitrary",), vmem_limit_bytes=VMEM_LIMIT),
        name="proj",
    )(x2, posc, invl, gmix, win, wvat, wwit, gcq, wuq, gckv, wukvk, wukvvt)


def _col_sum(a):
    r = a.shape[0]
    part = jnp.sum(a.reshape(r // SUBLANE, SUBLANE, a.shape[1]), axis=0)
    return jnp.sum(part, axis=0, keepdims=True)


def _col_max(a):
    r = a.shape[0]
    part = jnp.max(a.reshape(r // SUBLANE, SUBLANE, a.shape[1]), axis=0)
    return jnp.max(part, axis=0, keepdims=True)


def _attn_kernel(qa_ref, qi_ref, qb_ref, wit_ref, posq_ref,
                 kap_ref, kip_ref, kb_ref, vat_ref, vbt_ref, posk_ref,
                 yat_ref, ybt_ref,
                 score_ref, sb_ref, lg_ref, acc_ref, m_ref, mu_ref, *, topk):
    bf = jnp.bfloat16
    f32 = jnp.float32
    i = pl.program_id(1)
    nch = i + 1
    row_i = lax.broadcasted_iota(jnp.int32, (KC, TQ), 0)
    col_i = lax.broadcasted_iota(jnp.int32, (KC, TQ), 1)
    causal = row_i <= col_i
    kf = float(topk)

    def kslice(c):
        if isinstance(c, int):
            return pl.ds(c * KC, KC)
        return pl.ds(pl.multiple_of(c * KC, KC), KC)

    w = wit_ref[...]

    def score_chunk(c, carry):
        acc = jnp.zeros((KC, TQ), f32)
        for h in range(N_IDX_HEADS):
            kk = kip_ref[kslice(c), (h % 2) * LANE:(h % 2 + 1) * LANE]
            qq = qi_ref[:, (h // 2) * LANE:(h // 2 + 1) * LANE]
            d = lax.dot_general(kk, qq, _NT, preferred_element_type=f32)
            acc = acc + jnp.maximum(d, 0.0) * w[h:h + 1, :]
        score_ref[c] = acc
        sb_ref[c] = acc.astype(bf)
        return carry

    lax.fori_loop(0, nch, score_chunk, 0)
    diag = jnp.where(causal, score_ref[i], -jnp.inf)
    score_ref[i] = diag
    sb_ref[i] = diag.astype(bf)

    def count_rounded(thr_b):
        def body(c, acc):
            hit = (sb_ref[c] >= thr_b).reshape(KC // COUNT_ROWS, COUNT_ROWS, TQ)
            for r in range(KC // COUNT_ROWS):
                acc = jnp.where(hit[r], acc + 1.0, acc)
            return acc
        acc = lax.fori_loop(0, nch, body, jnp.zeros((COUNT_ROWS, TQ), bf))
        return _col_sum(acc.astype(f32))

    def key16_to_bf16(key):
        bits = jnp.where(key >= 0, key, key ^ jnp.int32(0x7FFF))
        return lax.bitcast_convert_type(lax.shift_left(bits, jnp.int32(16)), f32).astype(bf)

    def coarse_step(j, key):
        cand = key + lax.shift_left(jnp.int32(1), jnp.int32(15) - j)
        cnt = count_rounded(key16_to_bf16(cand))
        return jnp.where(cnt >= kf, cand, key)

    key16 = lax.fori_loop(0, 16, coarse_step, jnp.full((1, TQ), -(1 << 15), jnp.int32))
    fine_lo = lax.shift_left(key16 - 1, jnp.int32(16))

    def count(pred):
        def body(c, acc):
            hit = pred(score_ref[c]).reshape(KC // COUNT_ROWS, COUNT_ROWS, TQ)
            for r in range(KC // COUNT_ROWS):
                acc = jnp.where(hit[r], acc + 1.0, acc)
            return acc
        acc = lax.fori_loop(0, nch, body, jnp.zeros((COUNT_ROWS, TQ), f32))
        return _col_sum(acc)

    def key_to_float(key):
        bits = jnp.where(key >= 0, key, key ^ jnp.int32(0x7FFFFFFF))
        return lax.bitcast_convert_type(bits, f32)

    def fine_step(j, off):
        cand = off + lax.shift_left(jnp.int32(1), jnp.int32(FINE_BITS - 1) - j)
        thr_c = key_to_float(fine_lo + cand)
        cnt = count(lambda s: s >= thr_c)
        return jnp.where(cnt >= kf, cand, off)

    key = fine_lo + lax.fori_loop(0, FINE_BITS, fine_step, jnp.zeros((1, TQ), jnp.int32))
    t_glob = i * TQ + lax.broadcasted_iota(jnp.int32, (1, TQ), 1)
    take_all = t_glob < topk
    thr = jnp.where(take_all, jnp.finfo(f32).min, key_to_float(key))
    n_ge = count(lambda s: s >= thr)
    surplus = jnp.where(take_all, 0.0, n_ge - kf)
    posq = posq_ref[...]

    any_surplus = jnp.max(surplus) > 0.0

    def distance(c):
        pk = posk_ref[kslice(c), :]
        return jnp.abs(jnp.concatenate([pk] * (TQ // LANE), axis=1) - posq)

    @pl.when(jnp.logical_not(any_surplus))
    def _select_all_ties():
        def chunk(c, carry):
            score_ref[c] = jnp.where(score_ref[c] >= thr, distance(c), FAR)
            return carry

        lax.fori_loop(0, nch, chunk, 0)

    @pl.when(any_surplus)
    def _select_ranked_ties():
        n_gt = count(lambda s: s > thr)
        need = jnp.where(take_all, float(score_ref.shape[0] * KC + 1), kf - n_gt)
        tri = jnp.where(lax.broadcasted_iota(jnp.int32, (KC, KC), 0)
                        >= lax.broadcasted_iota(jnp.int32, (KC, KC), 1), 1.0, 0.0).astype(bf)

        def chunk(c, taken):
            s = score_ref[c]
            eq = s == thr
            pc = jnp.dot(tri, jnp.where(eq, 1.0, 0.0).astype(bf),
                         preferred_element_type=f32) + taken
            rank = jnp.where(eq, pc, jnp.where(s > thr, -jnp.inf, jnp.inf))
            score_ref[c] = jnp.where(rank <= need, distance(c), FAR)
            return pc[KC - 1:KC, :]

        lax.fori_loop(0, nch, chunk, jnp.zeros((1, TQ), f32))

    def dsa_logits(c, h, first):
        g = h // GROUP_A
        kk = kap_ref[kslice(c), (2 * g + h % 2) * LANE:(2 * g + h % 2 + 1) * LANE]
        qq = qa_ref[:, (h // 2) * LANE:(h // 2 + 1) * LANE]
        slope = 2.0 ** (-8.0 * (h + 1) / N_HEADS_A) * LOG2E
        qk = lax.dot_general(kk, qq, _NT, preferred_element_type=f32)
        return qk - slope * score_ref[c]

    def dsa_values(c, h):
        g = h // GROUP_A
        return vat_ref[c, g * V_ROWS:(g + 1) * V_ROWS, :]

    def mla_chunk_of(j):
        return jnp.where(j == 0, i, j - 1)

    def mla_logits(c, h, first):
        kk = kb_ref[kslice(c), h * HEAD_SLOT:(h + 1) * HEAD_SLOT]
        qq = qb_ref[:, h * HEAD_SLOT:(h + 1) * HEAD_SLOT]
        qk = lax.dot_general(kk, qq, _NT, preferred_element_type=f32)
        return jnp.where(causal, qk, NEG) if first else qk

    def mla_values(c, h):
        return vbt_ref[c, h * V_ROWS:(h + 1) * V_ROWS, :]

    n_slots = N_HEADS_A + N_HEADS_B
    m_ref[...] = jnp.full(m_ref.shape, NEG, f32)
    mu_ref[...] = jnp.full(mu_ref.shape, NEG, f32)
    acc_ref[...] = jnp.zeros(acc_ref.shape, f32)

    def slot_logits(j, s, first):
        if s < N_HEADS_A:
            return dsa_logits(j, s, first)
        return mla_logits(mla_chunk_of(j), s - N_HEADS_A, first)

    def slot_values(j, s):
        if s < N_HEADS_A:
            return dsa_values(j, s)
        return mla_values(mla_chunk_of(j), s - N_HEADS_A)

    def stage(j, s, first):
        lg = slot_logits(j, s, first)
        lg_ref[s] = lg
        m_ref[s:s + 1, :] = jnp.maximum(m_ref[s:s + 1, :], _col_max(lg))

    def consume(j, s):
        m_now = m_ref[s:s + 1, :]
        alpha = jnp.exp2(mu_ref[s:s + 1, :] - m_now)
        mu_ref[s:s + 1, :] = m_now
        p = jnp.exp2(lg_ref[s] - m_now)
        pv = jnp.dot(slot_values(j, s), p.astype(bf), preferred_element_type=f32)
        acc_ref[s] = alpha * acc_ref[s] + pv

    for s in range(n_slots):
        stage(0, s, True)

    def sweep(j, carry):
        for s in range(n_slots):
            consume(j, s)
            stage(j + 1, s, False)
        return carry

    lax.fori_loop(0, i, sweep, 0)
    for s in range(n_slots):
        consume(i, s)
        out_ref, h, dv = (yat_ref, s, HEAD_DIM_A) if s < N_HEADS_A else (ybt_ref, s - N_HEADS_A, V_DIM_B)
        inv_l = 1.0 / acc_ref[s, dv:dv + 1, :]
        out_ref[h * dv:(h + 1) * dv, :] = (acc_ref[s, :dv, :] * inv_l).astype(out_ref.dtype)


def _attn_call(qa, qi, qb, wit, posr, kap, kip, kb, vat, vbt, posc, batch, seq, topk):
    nq = seq // TQ
    bf = jnp.bfloat16
    n_tok = batch * seq
    n_slots = N_HEADS_A + N_HEADS_B
    assert HEAD_DIM_A == V_DIM_B
    qrow = lambda w: pl.BlockSpec((TQ, w), lambda b, i: (b * nq + i, 0))
    qcol = lambda r: pl.BlockSpec((r, TQ), lambda b, i: (0, b * nq + i))
    krow = lambda w: pl.BlockSpec((seq, w), lambda b, i: (b, 0))
    kchunks = lambda r: pl.BlockSpec((seq // KC, r, KC), lambda b, i: (b, 0, 0))
    in_specs = [qrow(qa.shape[1]), qrow(qi.shape[1]), qrow(qb.shape[1]), qcol(SUBLANE), qcol(1),
                krow(kap.shape[1]), krow(kip.shape[1]), krow(kb.shape[1]),
                kchunks(vat.shape[1]), kchunks(vbt.shape[1]), krow(LANE)]
    out_shape = [jax.ShapeDtypeStruct((N_HEADS_A * HEAD_DIM_A, n_tok), bf),
                 jax.ShapeDtypeStruct((N_HEADS_B * V_DIM_B, n_tok), bf)]
    out_specs = [qcol(N_HEADS_A * HEAD_DIM_A), qcol(N_HEADS_B * V_DIM_B)]
    scratch = [pltpu.VMEM((seq // KC, KC, TQ), jnp.float32),
               pltpu.VMEM((seq // KC, KC, TQ), bf),
               pltpu.VMEM((n_slots, KC, TQ), jnp.float32),
               pltpu.VMEM((n_slots, V_ROWS, TQ), jnp.float32),
               pltpu.VMEM((n_slots, TQ), jnp.float32),
               pltpu.VMEM((n_slots, TQ), jnp.float32)]
    return pl.pallas_call(
        functools.partial(_attn_kernel, topk=topk),
        grid=(batch, nq),
        in_specs=in_specs,
        out_specs=out_specs,
        out_shape=out_shape,
        scratch_shapes=scratch,
        compiler_params=pltpu.CompilerParams(
            dimension_semantics=("arbitrary", "arbitrary"), vmem_limit_bytes=VMEM_LIMIT),
        name="attn",
    )(qa, qi, qb, wit, posr, kap, kip, kb, vat, vbt, posc)


def _post_kernel(x_ref, yat_ref, ybt_ref, sga_ref, sgb_ref, p_ref,
                 wpa_ref, wpb_ref, wout_ref, gmlp_ref, wff1_ref, wff2_ref, gple_ref,
                 wpg_ref, wple_ref, gfin_ref, o_ref, *, final_norm):
    bf = jnp.bfloat16
    f32 = jnp.float32
    a = lax.dot_general(yat_ref[...], wpa_ref[...], _TN, preferred_element_type=f32)
    b = lax.dot_general(ybt_ref[...], wpb_ref[...], _TN, preferred_element_type=f32)
    merged = sga_ref[...].astype(f32) * a + sgb_ref[...].astype(f32) * b
    h = x_ref[...] + jnp.dot(merged.astype(bf), wout_ref[...], preferred_element_type=f32)
    n2 = _rms(h, gmlp_ref[...]).astype(bf)
    ff = jnp.zeros_like(h)
    for c in range(D_FF // FF_CHUNK):
        sl = slice(c * FF_CHUNK, (c + 1) * FF_CHUNK)
        u = jnp.maximum(jnp.dot(n2, wff1_ref[:, sl], preferred_element_type=f32), 0.0)
        ff = ff + jnp.dot((u * u).astype(bf), wff2_ref[sl, :], preferred_element_type=f32)
    h = h + ff
    n3 = _rms(h, gple_ref[...]).astype(bf)
    gate = jax.nn.sigmoid(jnp.dot(n3, wpg_ref[...], preferred_element_type=f32))
    pe = jnp.dot(p_ref[...].astype(bf), wple_ref[...], preferred_element_type=f32)
    h = h + gate * pe
    o_ref[...] = _rms(h, gfin_ref[...]) if final_norm else h


def _post_call(x2, yat, ybt, sga, sgb, p2, wpa, wpb, wout, gmlp, wff1, wff2, gple, wpg, wple, gfin,
               final_norm):
    n_tok = x2.shape[0]
    tm = TM_POST
    row = lambda w: pl.BlockSpec((tm, w), lambda t: (t, 0))
    col = lambda r: pl.BlockSpec((r, tm), lambda t: (0, t))
    wspec = lambda a: pl.BlockSpec(a.shape, lambda t: (0,) * a.ndim, pipeline_mode=pl.Buffered(1))
    in_specs = [row(D_MODEL), col(yat.shape[0]), col(ybt.shape[0]), row(D_MODEL), row(D_MODEL),
                row(PLE_DIM)] + [wspec(a) for a in
                                 (wpa, wpb, wout, gmlp, wff1, wff2, gple, wpg, wple, gfin)]
    return pl.pallas_call(
        functools.partial(_post_kernel, final_norm=final_norm),
        grid=(n_tok // tm,),
        in_specs=in_specs,
        out_specs=row(D_MODEL),
        out_shape=jax.ShapeDtypeStruct((n_tok, D_MODEL), jnp.float32),
        compiler_params=pltpu.CompilerParams(
            dimension_semantics=("arbitrary",), vmem_limit_bytes=VMEM_LIMIT),
        name="post",
    )(x2, yat, ybt, sga, sgb, p2, wpa, wpb, wout, gmlp, wff1, wff2, gple, wpg, wple, gfin)


def _pack_layer(w_in, w_uq, w_ukv):
    bf = jnp.bfloat16
    sizes = (N_HEADS_A * HEAD_DIM_A, N_KV_HEADS_A * HEAD_DIM_A, N_KV_HEADS_A * HEAD_DIM_A,
             N_IDX_HEADS * IDX_DIM, IDX_DIM, N_IDX_HEADS, Q_LORA, KV_LORA, ROPE_DIM,
             D_MODEL, D_MODEL)
    offs = np.concatenate([[0], np.cumsum(sizes)])
    wqa, wka, wva, wqi, wki, wwi, wcq, wckv, wkr, wga, wgb = [
        w_in[:, offs[j]:offs[j + 1]] for j in range(len(sizes))]
    d = w_in.shape[0]
    small_pad = jnp.zeros((d, LANE - IDX_DIM - ROPE_DIM), w_in.dtype)
    win = jnp.concatenate([wqa, wqi, wka, wki, wkr, small_pad, wcq, wckv, wga, wgb],
                          axis=1).astype(bf)
    wvat = wva.T.astype(bf)
    wwit = jnp.concatenate([wwi.T, jnp.zeros((SUBLANE - N_IDX_HEADS, d), w_in.dtype)],
                           axis=0).astype(bf)
    uq = w_uq.reshape(Q_LORA, N_HEADS_B, NOPE_DIM + ROPE_DIM)
    uq = jnp.pad(uq, ((0, 0), (0, 0), (0, HEAD_SLOT - NOPE_DIM - ROPE_DIM)))
    wuq = uq.reshape(Q_LORA, N_HEADS_B * HEAD_SLOT).astype(bf)
    ukv = w_ukv.reshape(KV_LORA, N_HEADS_B, NOPE_DIM + V_DIM_B)
    uk = jnp.pad(ukv[:, :, :NOPE_DIM], ((0, 0), (0, 0), (0, HEAD_SLOT - NOPE_DIM)))
    wukvk = uk.reshape(KV_LORA, N_HEADS_B * HEAD_SLOT).astype(bf)
    wukvvt = ukv[:, :, NOPE_DIM:].reshape(KV_LORA, N_HEADS_B * V_DIM_B).T.astype(bf)
    return win, wvat, wwit, wuq, wukvk, wukvvt


def kernel(x, p, positions, g_mix, w_in, g_cq, w_uq, g_ckv, w_ukv, w_pa, w_pb, w_out, g_mlp,
           w_ff1, w_ff2, g_ple, w_ple_gate, w_ple, g_final):
    bf = jnp.bfloat16
    batch, seq, d = x.shape
    depth = w_in.shape[0]
    n_tok = batch * seq
    topk = min(TOPK_MAX, seq // 4)
    assert d == D_MODEL and seq % TQ == 0 and TQ == KC and topk <= TQ and n_tok % TM_PROJ == 0
    assert IDX_DIM == NOPE_DIM == HEAD_DIM_A == LANE // 2

    posf = positions.astype(jnp.float32)
    posc = posf.reshape(n_tok, 1)
    posr = posf.reshape(1, n_tok)
    inv = ROPE_THETA ** (-jnp.arange(HALF_ROPE, dtype=jnp.float32) / HALF_ROPE)
    invl = jnp.concatenate([jnp.zeros((NOPE_DIM,), jnp.float32), inv, inv,
                            jnp.zeros((HEAD_SLOT - NOPE_DIM - ROPE_DIM,), jnp.float32)]).reshape(1, LANE)

    h = x.reshape(n_tok, d)
    for li in range(depth):
        win, wvat, wwit, wuq, wukvk, wukvvt = _pack_layer(w_in[li], w_uq[li], w_ukv[li])
        (qa, qi, qb, wit, kap, kip, kb, vat, vbt, sga, sgb, posb) = _proj_call(
            h, posc, invl, g_mix[li].reshape(1, d), win, wvat, wwit, g_cq[li].reshape(1, -1),
            wuq, g_ckv[li].reshape(1, -1), wukvk, wukvvt)
        yat, ybt = _attn_call(qa, qi, qb, wit, posr, kap, kip, kb, vat, vbt, posb, batch, seq, topk)
        h = _post_call(h, yat, ybt, sga, sgb, p[li].reshape(n_tok, -1),
                       w_pa[li].astype(bf), w_pb[li].astype(bf), w_out[li].astype(bf),
                       g_mlp[li].reshape(1, d), w_ff1[li].astype(bf), w_ff2[li].astype(bf),
                       g_ple[li].reshape(1, d), w_ple_gate[li].astype(bf), w_ple[li].astype(bf),
                       g_final.reshape(1, d), li == depth - 1)
    return h.reshape(batch, seq, d)
```

```python
import functools

import jax
import jax.numpy as jnp
import numpy as np
from jax import lax
from jax.experimental import pallas as pl
from jax.experimental.pallas import tpu as pltpu

D_MODEL = 1024
N_HEADS_A = 8
N_KV_HEADS_A = 2
HEAD_DIM_A = 64
GROUP_A = N_HEADS_A // N_KV_HEADS_A
N_IDX_HEADS = 4
IDX_DIM = 64
TOPK_MAX = 256
N_HEADS_B = 8
Q_LORA = 384
KV_LORA = 256
NOPE_DIM = 64
ROPE_DIM = 32
HALF_ROPE = ROPE_DIM // 2
V_DIM_B = 64
ROPE_THETA = 10000.0
D_FF = 4 * D_MODEL
PLE_DIM = 256
EPS = 1e-6

LANE = 128
SUBLANE = 8
HEAD_SLOT = 128
V_ROWS = V_DIM_B + 2 * SUBLANE
VMEM_LIMIT = 56 * 1024 * 1024

TM_PROJ = 512
TM_POST = 512
TQ = 256
KC = 256
FF_CHUNK = 1024
NEG = -1e30
FAR = 1e30
LOG2E = 1.4426950408889634
COUNT_ROWS = 32
FINE_BITS = 17

C_QA = 0
C_QI = C_QA + N_HEADS_A * HEAD_DIM_A
C_KA = C_QI + N_IDX_HEADS * IDX_DIM
C_SM = C_KA + N_KV_HEADS_A * HEAD_DIM_A
C_CQ = C_SM + LANE
C_CKV = C_CQ + Q_LORA
C_GA = C_CKV + KV_LORA
C_GB = C_GA + D_MODEL
C_END = C_GB + D_MODEL
KA_PAD = 2 * N_KV_HEADS_A * LANE
KI_PAD = 2 * LANE

_NT = (((1,), (1,)), ((), ()))
_TN = (((0,), (0,)), ((), ()))


def _rms(x, g):
    return x * lax.rsqrt(jnp.mean(x * x, axis=-1, keepdims=True) + EPS) * g


def _rope_slot(x, cos_t, sin_dn, sin_up):
    up = pltpu.roll(x, HALF_ROPE, 1)
    dn = pltpu.roll(x, LANE - HALF_ROPE, 1)
    return x * cos_t + dn * sin_dn + up * sin_up


def _proj_kernel(x_ref, pos_ref, invl_ref, gmix_ref, win_ref, wvat_ref, wwit_ref, gcq_ref,
                 wuq_ref, gckv_ref, wukvk_ref, wukvvt_ref,
                 qa_ref, qi_ref, qb_ref, wit_ref, kap_ref, kip_ref, kb_ref, vat_ref, vbt_ref,
                 sga_ref, sgb_ref, posb_ref):
    bf = jnp.bfloat16
    n = _rms(x_ref[...], gmix_ref[...]).astype(bf)
    posb_ref[...] = jnp.broadcast_to(pos_ref[...], posb_ref.shape)

    def cols(c0, c1):
        return jnp.dot(n, win_ref[:, c0:c1], preferred_element_type=jnp.float32)

    cqn = _rms(cols(C_CQ, C_CKV), gcq_ref[...]).astype(bf)
    ckvn = _rms(cols(C_CKV, C_GA), gckv_ref[...]).astype(bf)

    lane = lax.broadcasted_iota(jnp.int32, (1, LANE), 1)
    is_x1 = (lane >= NOPE_DIM) & (lane < NOPE_DIM + HALF_ROPE)
    is_x2 = (lane >= NOPE_DIM + HALF_ROPE) & (lane < NOPE_DIM + ROPE_DIM)
    ang = pos_ref[...] * invl_ref[...]
    cos, sin = jnp.cos(ang), jnp.sin(ang)
    cos_t = jnp.where(is_x1 | is_x2, cos, 1.0)
    sin_dn = jnp.where(is_x1, -sin, 0.0)
    sin_up = jnp.where(is_x2, sin, 0.0)

    qa_ref[...] = (cols(C_QA, C_QI) * (HEAD_DIM_A ** -0.5 * LOG2E)).astype(bf)
    qi_ref[...] = cols(C_QI, C_KA).astype(bf)
    low = lane < LANE // 2

    def half_slabs(z):
        swapped = pltpu.roll(z, LANE // 2, 1)
        return (jnp.where(low, z, 0.0), jnp.where(low, 0.0, swapped),
                jnp.where(low, swapped, 0.0), jnp.where(low, 0.0, z))

    for j, slab in enumerate(half_slabs(cols(C_KA, C_SM))):
        kap_ref[:, j * LANE:(j + 1) * LANE] = slab.astype(bf)
    small = cols(C_SM, C_CQ)
    ki_slabs = half_slabs(small)
    kip_ref[:, :LANE] = ki_slabs[0].astype(bf)
    kip_ref[:, LANE:] = ki_slabs[1].astype(bf)
    sga_ref[...] = jax.nn.sigmoid(cols(C_GA, C_GB)).astype(bf)
    sgb_ref[...] = jax.nn.sigmoid(cols(C_GB, C_END)).astype(bf)
    def put_chunks(ref, vt):
        ones = jnp.ones((V_ROWS - V_DIM_B, KC), bf)
        for j in range(TM_PROJ // KC):
            for g in range(vt.shape[0] // V_DIM_B):
                ref[j, g * V_ROWS:g * V_ROWS + V_DIM_B, :] = (
                    vt[g * V_DIM_B:(g + 1) * V_DIM_B, j * KC:(j + 1) * KC].astype(bf))
                ref[j, g * V_ROWS + V_DIM_B:(g + 1) * V_ROWS, :] = ones

    put_chunks(vat_ref, lax.dot_general(wvat_ref[...], n, _NT, preferred_element_type=jnp.float32))
    wit_ref[...] = lax.dot_general(wwit_ref[...], n, _NT,
                                   preferred_element_type=jnp.float32) * ((N_IDX_HEADS * IDX_DIM) ** -0.5)

    qscale = (NOPE_DIM + ROPE_DIM) ** -0.5 * LOG2E
    qall = jnp.dot(cqn, wuq_ref[...], preferred_element_type=jnp.float32)
    for h in range(N_HEADS_B):
        sl = slice(h * HEAD_SLOT, (h + 1) * HEAD_SLOT)
        qb_ref[:, sl] = (_rope_slot(qall[:, sl], cos_t, sin_dn, sin_up) * qscale).astype(bf)

    kr =_rope_slot(jnp.where(low, 0.0, small), cos_t, sin_dn, sin_up)
    kall = jnp.dot(ckvn, wukvk_ref[...], preferred_element_type=jnp.float32)
    for h in range(N_HEADS_B):
        sl = slice(h * HEAD_SLOT, (h + 1) * HEAD_SLOT)
        kb_ref[:, sl] = (kall[:, sl] + kr).astype(bf)
    put_chunks(vbt_ref, lax.dot_general(wukvvt_ref[...], ckvn, _NT,
                                        preferred_element_type=jnp.float32))


def _const_spec(shape):
    return pl.BlockSpec(shape, lambda *_: (0,) * len(shape))


def _proj_call(x2, posc, invl, gmix, win, wvat, wwit, gcq, wuq, gckv, wukvk, wukvvt):
    n_tok = x2.shape[0]
    tm = TM_PROJ
    bf = jnp.bfloat16
    row = lambda w: pl.BlockSpec((tm, w), lambda t: (t, 0))
    col = lambda r: pl.BlockSpec((r, tm), lambda t: (0, t))
    chunks = lambda r: pl.BlockSpec((tm // KC, r, KC), lambda t: (t, 0, 0))
    in_specs = [row(D_MODEL), row(1), _const_spec(invl.shape), _const_spec(gmix.shape),
                _const_spec(win.shape), _const_spec(wvat.shape), _const_spec(wwit.shape),
                _const_spec(gcq.shape), _const_spec(wuq.shape), _const_spec(gckv.shape),
                _const_spec(wukvk.shape), _const_spec(wukvvt.shape)]
    out_shape = [
        jax.ShapeDtypeStruct((n_tok, C_QI - C_QA), bf),
        jax.ShapeDtypeStruct((n_tok, C_KA - C_QI), bf),
        jax.ShapeDtypeStruct((n_tok, N_HEADS_B * HEAD_SLOT), bf),
        jax.ShapeDtypeStruct((SUBLANE, n_tok), jnp.float32),
        jax.ShapeDtypeStruct((n_tok, KA_PAD), bf),
        jax.ShapeDtypeStruct((n_tok, KI_PAD), bf),
        jax.ShapeDtypeStruct((n_tok, N_HEADS_B * HEAD_SLOT), bf),
        jax.ShapeDtypeStruct((n_tok // KC, N_KV_HEADS_A * V_ROWS, KC), bf),
        jax.ShapeDtypeStruct((n_tok // KC, N_HEADS_B * V_ROWS, KC), bf),
        jax.ShapeDtypeStruct((n_tok, D_MODEL), bf),
        jax.ShapeDtypeStruct((n_tok, D_MODEL), bf),
        jax.ShapeDtypeStruct((n_tok, LANE), jnp.float32),
    ]
    out_specs = [row(C_QI - C_QA), row(C_KA - C_QI), row(N_HEADS_B * HEAD_SLOT), col(SUBLANE),
                 row(KA_PAD), row(KI_PAD), row(N_HEADS_B * HEAD_SLOT),
                 chunks(N_KV_HEADS_A * V_ROWS), chunks(N_HEADS_B * V_ROWS),
                 row(D_MODEL), row(D_MODEL), row(LANE)]
    return pl.pallas_call(
        _proj_kernel,
        grid=(n_tok // tm,),
        in_specs=in_specs,
        out_specs=out_specs,
        out_shape=out_shape,
        compiler_params=pltpu.CompilerParams(
            dimension_semantics=("arbitrary",), vmem_limit_bytes=VMEM_LIMIT),
        name="proj",
    )(x2, posc, invl, gmix, win, wvat, wwit, gcq, wuq, gckv, wukvk, wukvvt)


def _col_sum(a):
    r = a.shape[0]
    part = jnp.sum(a.reshape(r // SUBLANE, SUBLANE, a.shape[1]), axis=0)
    return jnp.sum(part, axis=0, keepdims=True)


def _col_max(a):
    r = a.shape[0]
    part = jnp.max(a.reshape(r // SUBLANE, SUBLANE, a.shape[1]), axis=0)
    return jnp.max(part, axis=0, keepdims=True)


def _attn_kernel(qa_ref, qi_ref, qb_ref, wit_ref, posq_ref,
                 kap_ref, kip_ref, kb_ref, vat_ref, vbt_ref, posk_ref,
                 yat_ref, ybt_ref,
                 score_ref, sb_ref, lg_ref, acc_ref, m_ref, mu_ref, *, topk):
    bf = jnp.bfloat16
    f32 = jnp.float32
    i = pl.program_id(1)
    nch = i + 1
    row_i = lax.broadcasted_iota(jnp.int32, (KC, TQ), 0)
    col_i = lax.broadcasted_iota(jnp.int32, (KC, TQ), 1)
    causal = row_i <= col_i
    kf = float(topk)

    def kslice(c):
        if isinstance(c, int):
            return pl.ds(c * KC, KC)
        return pl.ds(pl.multiple_of(c * KC, KC), KC)

    w = wit_ref[...]

    def score_chunk(c, carry):
        acc = jnp.zeros((KC, TQ), f32)
        for h in range(N_IDX_HEADS):
            kk = kip_ref[kslice(c), (h % 2) * LANE:(h % 2 + 1) * LANE]
            qq = qi_ref[:, (h // 2) * LANE:(h // 2 + 1) * LANE]
            d = lax.dot_general(kk, qq, _NT, preferred_element_type=f32)
            acc = acc + jnp.maximum(d, 0.0) * w[h:h + 1, :]
        score_ref[c] = acc
        sb_ref[c] = acc.astype(bf)
        return carry

    lax.fori_loop(0, nch, score_chunk, 0)
    diag = jnp.where(causal, score_ref[i], -jnp.inf)
    score_ref[i] = diag
    sb_ref[i] = diag.astype(bf)

    def count_rounded(thr_b):
        def body(c, acc):
            hit = (sb_ref[c] >= thr_b).reshape(KC // COUNT_ROWS, COUNT_ROWS, TQ)
            for r in range(KC // COUNT_ROWS):
                acc = jnp.where(hit[r], acc + 1.0, acc)
            return acc
        acc = lax.fori_loop(0, nch, body, jnp.zeros((COUNT_ROWS, TQ), bf))
        return _col_sum(acc.astype(f32))

    def key16_to_bf16(key):
        bits = jnp.where(key >= 0, key, key ^ jnp.int32(0x7FFF))
        return lax.bitcast_convert_type(lax.shift_left(bits, jnp.int32(16)), f32).astype(bf)

    def coarse_step(j, key):
        cand = key + lax.shift_left(jnp.int32(1), jnp.int32(15) - j)
        cnt = count_rounded(key16_to_bf16(cand))
        return jnp.where(cnt >= kf, cand, key)

    key16 = lax.fori_loop(0, 16, coarse_step, jnp.full((1, TQ), -(1 << 15), jnp.int32))
    fine_lo = lax.shift_left(key16 - jnp.where(key16 >= 0, 1, 0), jnp.int32(16))

    def count(pred):
        def body(c, acc):
            hit = pred(score_ref[c]).reshape(KC // COUNT_ROWS, COUNT_ROWS, TQ)
            for r in range(KC // COUNT_ROWS):
                acc = jnp.where(hit[r], acc + 1.0, acc)
            return acc
        acc = lax.fori_loop(0, nch, body, jnp.zeros((COUNT_ROWS, TQ), f32))
        return _col_sum(acc)

    def key_to_float(key):
        bits = jnp.where(key >= 0, key, key ^ jnp.int32(0x7FFFFFFF))
        return lax.bitcast_convert_type(bits, f32)

    def fine_step(j, off):
        cand = off + lax.shift_left(jnp.int32(1), jnp.int32(FINE_BITS - 1) - j)
        thr_c = key_to_float(fine_lo + cand)
        cnt = count(lambda s: s >= thr_c)
        return jnp.where(cnt >= kf, cand, off)

    key = fine_lo + lax.fori_loop(0, FINE_BITS, fine_step, jnp.zeros((1, TQ), jnp.int32))
    t_glob = i * TQ + lax.broadcasted_iota(jnp.int32, (1, TQ), 1)
    take_all = t_glob < topk
    thr = jnp.where(take_all, jnp.finfo(f32).min, key_to_float(key))
    n_ge = count(lambda s: s >= thr)
    surplus = jnp.where(take_all, 0.0, n_ge - kf)
    posq = posq_ref[...]

    any_surplus = jnp.max(surplus) > 0.0

    def distance(c):
        pk = posk_ref[kslice(c), :]
        return jnp.abs(jnp.concatenate([pk] * (TQ // LANE), axis=1) - posq)

    @pl.when(jnp.logical_not(any_surplus))
    def _select_all_ties():
        def chunk(c, carry):
            score_ref[c] = jnp.where(score_ref[c] >= thr, distance(c), FAR)
            return carry

        lax.fori_loop(0, nch, chunk, 0)

    @pl.when(any_surplus)
    def _select_ranked_ties():
        n_gt = count(lambda s: s > thr)
        need = jnp.where(take_all, float(score_ref.shape[0] * KC + 1), kf - n_gt)
        tri = jnp.where(lax.broadcasted_iota(jnp.int32, (KC, KC), 0)
                        >= lax.broadcasted_iota(jnp.int32, (KC, KC), 1), 1.0, 0.0).astype(bf)

        def chunk(c, taken):
            s = score_ref[c]
            eq = s == thr
            pc = jnp.dot(tri, jnp.where(eq, 1.0, 0.0).astype(bf),
                         preferred_element_type=f32) + taken
            rank = jnp.where(eq, pc, jnp.where(s > thr, -jnp.inf, jnp.inf))
            score_ref[c] = jnp.where(rank <= need, distance(c), FAR)
            return pc[KC - 1:KC, :]

        lax.fori_loop(0, nch, chunk, jnp.zeros((1, TQ), f32))

    def dsa_logits(c, h, first):
        g = h // GROUP_A
        kk = kap_ref[kslice(c), (2 * g + h % 2) * LANE:(2 * g + h % 2 + 1) * LANE]
        qq = qa_ref[:, (h // 2) * LANE:(h // 2 + 1) * LANE]
        slope = 2.0 ** (-8.0 * (h + 1) / N_HEADS_A) * LOG2E
        qk = lax.dot_general(kk, qq, _NT, preferred_element_type=f32)
        return qk - slope * score_ref[c]

    def dsa_values(c, h):
        g = h // GROUP_A
        return vat_ref[c, g * V_ROWS:(g + 1) * V_ROWS, :]

    def mla_chunk_of(j):
        return jnp.where(j == 0, i, j - 1)

    def mla_logits(c, h, first):
        kk = kb_ref[kslice(c), h * HEAD_SLOT:(h + 1) * HEAD_SLOT]
        qq = qb_ref[:, h * HEAD_SLOT:(h + 1) * HEAD_SLOT]
        qk = lax.dot_general(kk, qq, _NT, preferred_element_type=f32)
        return jnp.where(causal, qk, NEG) if first else qk

    def mla_values(c, h):
        return vbt_ref[c, h * V_ROWS:(h + 1) * V_ROWS, :]

    n_slots = N_HEADS_A + N_HEADS_B
    m_ref[...] = jnp.full(m_ref.shape, NEG, f32)
    mu_ref[...] = jnp.full(mu_ref.shape, NEG, f32)
    acc_ref[...] = jnp.zeros(acc_ref.shape, f32)

    def slot_logits(j, s, first):
        if s < N_HEADS_A:
            return dsa_logits(j, s, first)
        return mla_logits(mla_chunk_of(j), s - N_HEADS_A, first)

    def slot_values(j, s):
        if s < N_HEADS_A:
            return dsa_values(j, s)
        return mla_values(mla_chunk_of(j), s - N_HEADS_A)

    def stage(j, s, first):
        lg = slot_logits(j, s, first)
        lg_ref[s] = lg
        m_ref[s:s + 1, :] = jnp.maximum(m_ref[s:s + 1, :], _col_max(lg))

    def consume(j, s):
        m_now = m_ref[s:s + 1, :]
        alpha = jnp.exp2(mu_ref[s:s + 1, :] - m_now)
        mu_ref[s:s + 1, :] = m_now
        p = jnp.exp2(lg_ref[s] - m_now)
        pv = jnp.dot(slot_values(j, s), p.astype(bf), preferred_element_type=f32)
        acc_ref[s] = alpha * acc_ref[s] + pv

    slot_order = [s for pair in zip(range(N_HEADS_A), range(N_HEADS_A, n_slots)) for s in pair]
    for s in slot_order:
        stage(0, s, True)

    def sweep(j, carry):
        for s in slot_order:
            consume(j, s)
            stage(j + 1, s, False)
        return carry

    lax.fori_loop(0, i, sweep, 0)
    for s in range(n_slots):
        consume(i, s)
        out_ref, h, dv = (yat_ref, s, HEAD_DIM_A) if s < N_HEADS_A else (ybt_ref, s - N_HEADS_A, V_DIM_B)
        inv_l = 1.0 / acc_ref[s, dv:dv + 1, :]
        out_ref[h * dv:(h + 1) * dv, :] = (acc_ref[s, :dv, :] * inv_l).astype(out_ref.dtype)


def _attn_call(qa, qi, qb, wit, posr, kap, kip, kb, vat, vbt, posc, batch, seq, topk):
    nq = seq // TQ
    bf = jnp.bfloat16
    n_tok = batch * seq
    n_slots = N_HEADS_A + N_HEADS_B
    assert HEAD_DIM_A == V_DIM_B
    qrow = lambda w: pl.BlockSpec((TQ, w), lambda b, i: (b * nq + i, 0))
    qcol = lambda r: pl.BlockSpec((r, TQ), lambda b, i: (0, b * nq + i))
    krow = lambda w: pl.BlockSpec((seq, w), lambda b, i: (b, 0))
    kchunks = lambda r: pl.BlockSpec((seq // KC, r, KC), lambda b, i: (b, 0, 0))
    in_specs = [qrow(qa.shape[1]), qrow(qi.shape[1]), qrow(qb.shape[1]), qcol(SUBLANE), qcol(1),
                krow(kap.shape[1]), krow(kip.shape[1]), krow(kb.shape[1]),
                kchunks(vat.shape[1]), kchunks(vbt.shape[1]), krow(LANE)]
    out_shape = [jax.ShapeDtypeStruct((N_HEADS_A * HEAD_DIM_A, n_tok), bf),
                 jax.ShapeDtypeStruct((N_HEADS_B * V_DIM_B, n_tok), bf)]
    out_specs = [qcol(N_HEADS_A * HEAD_DIM_A), qcol(N_HEADS_B * V_DIM_B)]
    scratch = [pltpu.VMEM((seq // KC, KC, TQ), jnp.float32),
               pltpu.VMEM((seq // KC, KC, TQ), bf),
               pltpu.VMEM((n_slots, KC, TQ), jnp.float32),
               pltpu.VMEM((n_slots, V_ROWS, TQ), jnp.float32),
               pltpu.VMEM((n_slots, TQ), jnp.float32),
               pltpu.VMEM((n_slots, TQ), jnp.float32)]
    return pl.pallas_call(
        functools.partial(_attn_kernel, topk=topk),
        grid=(batch, nq),
        in_specs=in_specs,
        out_specs=out_specs,
        out_shape=out_shape,
        scratch_shapes=scratch,
        compiler_params=pltpu.CompilerParams(
            dimension_semantics=("arbitrary", "arbitrary"), vmem_limit_bytes=VMEM_LIMIT),
        name="attn",
    )(qa, qi, qb, wit, posr, kap, kip, kb, vat, vbt, posc)


def _post_kernel(x_ref, yat_ref, ybt_ref, sga_ref, sgb_ref, p_ref,
                 wpa_ref, wpb_ref, wout_ref, gmlp_ref, wff1_ref, wff2_ref, gple_ref,
                 wpg_ref, wple_ref, gfin_ref, o_ref, *, final_norm):
    bf = jnp.bfloat16
    f32 = jnp.float32
    a = lax.dot_general(yat_ref[...], wpa_ref[...], _TN, preferred_element_type=f32)
    b = lax.dot_general(ybt_ref[...], wpb_ref[...], _TN, preferred_element_type=f32)
    merged = sga_ref[...].astype(f32) * a + sgb_ref[...].astype(f32) * b
    h = x_ref[...] + jnp.dot(merged.astype(bf), wout_ref[...], preferred_element_type=f32)
    n2 = _rms(h, gmlp_ref[...]).astype(bf)
    ff = jnp.zeros_like(h)
    for c in range(D_FF // FF_CHUNK):
        sl = slice(c * FF_CHUNK, (c + 1) * FF_CHUNK)
        u = jnp.maximum(jnp.dot(n2, wff1_ref[:, sl], preferred_element_type=f32), 0.0)
        ff = ff + jnp.dot((u * u).astype(bf), wff2_ref[sl, :], preferred_element_type=f32)
    h = h + ff
    n3 = _rms(h, gple_ref[...]).astype(bf)
    gate = jax.nn.sigmoid(jnp.dot(n3, wpg_ref[...], preferred_element_type=f32))
    pe = jnp.dot(p_ref[...].astype(bf), wple_ref[...], preferred_element_type=f32)
    h = h + gate * pe
    o_ref[...] = _rms(h, gfin_ref[...]) if final_norm else h


def _post_call(x2, yat, ybt, sga, sgb, p2, wpa, wpb, wout, gmlp, wff1, wff2, gple, wpg, wple, gfin,
               final_norm):
    n_tok = x2.shape[0]
    tm = TM_POST
    row = lambda w: pl.BlockSpec((tm, w), lambda t: (t, 0))
    col = lambda r: pl.BlockSpec((r, tm), lambda t: (0, t))
    wspec = lambda a: pl.BlockSpec(a.shape, lambda t: (0,) * a.ndim, pipeline_mode=pl.Buffered(1))
    in_specs = [row(D_MODEL), col(yat.shape[0]), col(ybt.shape[0]), row(D_MODEL), row(D_MODEL),
                row(PLE_DIM)] + [wspec(a) for a in
                                 (wpa, wpb, wout, gmlp, wff1, wff2, gple, wpg, wple, gfin)]
    return pl.pallas_call(
        functools.partial(_post_kernel, final_norm=final_norm),
        grid=(n_tok // tm,),
        in_specs=in_specs,
        out_specs=row(D_MODEL),
        out_shape=jax.ShapeDtypeStruct((n_tok, D_MODEL), jnp.float32),
        compiler_params=pltpu.CompilerParams(
            dimension_semantics=("arbitrary",), vmem_limit_bytes=VMEM_LIMIT),
        name="post",
    )(x2, yat, ybt, sga, sgb, p2, wpa, wpb, wout, gmlp, wff1, wff2, gple, wpg, wple, gfin)


def _pack_layer(w_in, w_uq, w_ukv):
    bf = jnp.bfloat16
    sizes = (N_HEADS_A * HEAD_DIM_A, N_KV_HEADS_A * HEAD_DIM_A, N_KV_HEADS_A * HEAD_DIM_A,
             N_IDX_HEADS * IDX_DIM, IDX_DIM, N_IDX_HEADS, Q_LORA, KV_LORA, ROPE_DIM,
             D_MODEL, D_MODEL)
    offs = np.concatenate([[0], np.cumsum(sizes)])
    wqa, wka, wva, wqi, wki, wwi, wcq, wckv, wkr, wga, wgb = [
        w_in[:, offs[j]:offs[j + 1]] for j in range(len(sizes))]
    d = w_in.shape[0]
    small_pad = jnp.zeros((d, LANE - IDX_DIM - ROPE_DIM), w_in.dtype)
    win = jnp.concatenate([wqa, wqi, wka, wki, wkr, small_pad, wcq, wckv, wga, wgb],
                          axis=1).astype(bf)
    wvat = wva.T.astype(bf)
    wwit = jnp.concatenate([wwi.T, jnp.zeros((SUBLANE - N_IDX_HEADS, d), w_in.dtype)],
                           axis=0).astype(bf)
    uq = w_uq.reshape(Q_LORA, N_HEADS_B, NOPE_DIM + ROPE_DIM)
    uq = jnp.pad(uq, ((0, 0), (0, 0), (0, HEAD_SLOT - NOPE_DIM - ROPE_DIM)))
    wuq = uq.reshape(Q_LORA, N_HEADS_B * HEAD_SLOT).astype(bf)
    ukv = w_ukv.reshape(KV_LORA, N_HEADS_B, NOPE_DIM + V_DIM_B)
    uk = jnp.pad(ukv[:, :, :NOPE_DIM], ((0, 0), (0, 0), (0, HEAD_SLOT - NOPE_DIM)))
    wukvk = uk.reshape(KV_LORA, N_HEADS_B * HEAD_SLOT).astype(bf)
    wukvvt = ukv[:, :, NOPE_DIM:].reshape(KV_LORA, N_HEADS_B * V_DIM_B).T.astype(bf)
    return win, wvat, wwit, wuq, wukvk, wukvvt


def kernel(x, p, positions, g_mix, w_in, g_cq, w_uq, g_ckv, w_ukv, w_pa, w_pb, w_out, g_mlp,
           w_ff1, w_ff2, g_ple, w_ple_gate, w_ple, g_final):
    bf = jnp.bfloat16
    batch, seq, d = x.shape
    depth = w_in.shape[0]
    n_tok = batch * seq
    topk = min(TOPK_MAX, seq // 4)
    assert d == D_MODEL and seq % TQ == 0 and TQ == KC and topk <= TQ and n_tok % TM_PROJ == 0
    assert IDX_DIM == NOPE_DIM == HEAD_DIM_A == LANE // 2

    posf = positions.astype(jnp.float32)
    posc = posf.reshape(n_tok, 1)
    posr = posf.reshape(1, n_tok)
    inv = ROPE_THETA ** (-jnp.arange(HALF_ROPE, dtype=jnp.float32) / HALF_ROPE)
    invl = jnp.concatenate([jnp.zeros((NOPE_DIM,), jnp.float32), inv, inv,
                            jnp.zeros((HEAD_SLOT - NOPE_DIM - ROPE_DIM,), jnp.float32)]).reshape(1, LANE)

    h = x.reshape(n_tok, d)
    for li in range(depth):
        win, wvat, wwit, wuq, wukvk, wukvvt = _pack_layer(w_in[li], w_uq[li], w_ukv[li])
        (qa, qi, qb, wit, kap, kip, kb, vat, vbt, sga, sgb, posb) = _proj_call(
            h, posc, invl, g_mix[li].reshape(1, d), win, wvat, wwit, g_cq[li].reshape(1, -1),
            wuq, g_ckv[li].reshape(1, -1), wukvk, wukvvt)
        yat, ybt = _attn_call(qa, qi, qb, wit, posr, kap, kip, kb, vat, vbt, posb, batch, seq, topk)
        h = _post_call(h, yat, ybt, sga, sgb, p[li].reshape(n_tok, -1),
                       w_pa[li].astype(bf), w_pb[li].astype(bf), w_out[li].astype(bf),
                       g_mlp[li].reshape(1, d), w_ff1[li].astype(bf), w_ff2[li].astype(bf),
                       g_ple[li].reshape(1, d), w_ple_gate[li].astype(bf), w_ple[li].astype(bf),
                       g_final.reshape(1, d), li == depth - 1)
    return h.reshape(batch, seq, d)
```

```python
import functools

import jax
import jax.numpy as jnp
import numpy as np
from jax import lax
from jax.experimental import pallas as pl
from jax.experimental.pallas import tpu as pltpu

D_MODEL = 1024
N_HEADS_A = 8
N_KV_HEADS_A = 2
HEAD_DIM_A = 64
GROUP_A = N_HEADS_A // N_KV_HEADS_A
N_IDX_HEADS = 4
IDX_DIM = 64
TOPK_MAX = 256
N_HEADS_B = 8
Q_LORA = 384
KV_LORA = 256
NOPE_DIM = 64
ROPE_DIM = 32
HALF_ROPE = ROPE_DIM // 2
V_DIM_B = 64
ROPE_THETA = 10000.0
D_FF = 4 * D_MODEL
PLE_DIM = 256
EPS = 1e-6

LANE = 128
SUBLANE = 8
HEAD_SLOT = 128
V_ROWS = V_DIM_B + 2 * SUBLANE
VMEM_LIMIT = 56 * 1024 * 1024

TM_PROJ = 1024
TM_POST = 512
TQ = 256
KC = 256
FF_CHUNK = 1024
NEG = -1e30
FAR = 1e30
LOG2E = 1.4426950408889634
COUNT_ROWS = 32
FINE_BITS = 17

C_QA = 0
C_QI = C_QA + N_HEADS_A * HEAD_DIM_A
C_KA = C_QI + N_IDX_HEADS * IDX_DIM
C_SM = C_KA + N_KV_HEADS_A * HEAD_DIM_A
C_CQ = C_SM + LANE
C_CKV = C_CQ + Q_LORA
C_GA = C_CKV + KV_LORA
C_GB = C_GA + D_MODEL
C_END = C_GB + D_MODEL
KA_PAD = 2 * N_KV_HEADS_A * LANE
KI_PAD = 2 * LANE

_NT = (((1,), (1,)), ((), ()))
_TN = (((0,), (0,)), ((), ()))


def _rms(x, g):
    return x * lax.rsqrt(jnp.mean(x * x, axis=-1, keepdims=True) + EPS) * g


def _rope_slot(x, cos_t, sin_dn, sin_up):
    up = pltpu.roll(x, HALF_ROPE, 1)
    dn = pltpu.roll(x, LANE - HALF_ROPE, 1)
    return x * cos_t + dn * sin_dn + up * sin_up


def _proj_kernel(x_ref, pos_ref, invl_ref, gmix_ref, win_ref, wvat_ref, wwit_ref, gcq_ref,
                 wuq_ref, gckv_ref, wukvk_ref, wukvvt_ref,
                 qa_ref, qi_ref, qb_ref, wit_ref, kap_ref, kip_ref, kb_ref, vat_ref, vbt_ref,
                 sga_ref, sgb_ref, posb_ref):
    bf = jnp.bfloat16
    n = _rms(x_ref[...], gmix_ref[...]).astype(bf)
    posb_ref[...] = jnp.broadcast_to(pos_ref[...], posb_ref.shape)

    def cols(c0, c1):
        return jnp.dot(n, win_ref[:, c0:c1], preferred_element_type=jnp.float32)

    cqn = _rms(cols(C_CQ, C_CKV), gcq_ref[...]).astype(bf)
    ckvn = _rms(cols(C_CKV, C_GA), gckv_ref[...]).astype(bf)

    lane = lax.broadcasted_iota(jnp.int32, (1, LANE), 1)
    is_x1 = (lane >= NOPE_DIM) & (lane < NOPE_DIM + HALF_ROPE)
    is_x2 = (lane >= NOPE_DIM + HALF_ROPE) & (lane < NOPE_DIM + ROPE_DIM)
    ang = pos_ref[...] * invl_ref[...]
    cos, sin = jnp.cos(ang), jnp.sin(ang)
    cos_t = jnp.where(is_x1 | is_x2, cos, 1.0)
    sin_dn = jnp.where(is_x1, -sin, 0.0)
    sin_up = jnp.where(is_x2, sin, 0.0)

    qa_ref[...] = (cols(C_QA, C_QI) * (HEAD_DIM_A ** -0.5 * LOG2E)).astype(bf)
    qi_ref[...] = cols(C_QI, C_KA).astype(bf)
    low = lane < LANE // 2

    def half_slabs(z):
        swapped = pltpu.roll(z, LANE // 2, 1)
        return (jnp.where(low, z, 0.0), jnp.where(low, 0.0, swapped),
                jnp.where(low, swapped, 0.0), jnp.where(low, 0.0, z))

    for j, slab in enumerate(half_slabs(cols(C_KA, C_SM))):
        kap_ref[:, j * LANE:(j + 1) * LANE] = slab.astype(bf)
    small = cols(C_SM, C_CQ)
    ki_slabs = half_slabs(small)
    kip_ref[:, :LANE] = ki_slabs[0].astype(bf)
    kip_ref[:, LANE:] = ki_slabs[1].astype(bf)
    sga_ref[...] = jax.nn.sigmoid(cols(C_GA, C_GB)).astype(bf)
    sgb_ref[...] = jax.nn.sigmoid(cols(C_GB, C_END)).astype(bf)
    def put_chunks(ref, vt):
        ones = jnp.ones((V_ROWS - V_DIM_B, KC), bf)
        for j in range(TM_PROJ // KC):
            for g in range(vt.shape[0] // V_DIM_B):
                ref[j, g * V_ROWS:g * V_ROWS + V_DIM_B, :] = (
                    vt[g * V_DIM_B:(g + 1) * V_DIM_B, j * KC:(j + 1) * KC].astype(bf))
                ref[j, g * V_ROWS + V_DIM_B:(g + 1) * V_ROWS, :] = ones

    put_chunks(vat_ref, lax.dot_general(wvat_ref[...], n, _NT, preferred_element_type=jnp.float32))
    wit_ref[...] = lax.dot_general(wwit_ref[...], n, _NT,
                                   preferred_element_type=jnp.float32) * ((N_IDX_HEADS * IDX_DIM) ** -0.5)

    qscale = (NOPE_DIM + ROPE_DIM) ** -0.5 * LOG2E
    qall = jnp.dot(cqn, wuq_ref[...], preferred_element_type=jnp.float32)
    for h in range(N_HEADS_B):
        sl = slice(h * HEAD_SLOT, (h + 1) * HEAD_SLOT)
        qb_ref[:, sl] = (_rope_slot(qall[:, sl], cos_t, sin_dn, sin_up) * qscale).astype(bf)

    kr =_rope_slot(jnp.where(low, 0.0, small), cos_t, sin_dn, sin_up)
    kall = jnp.dot(ckvn, wukvk_ref[...], preferred_element_type=jnp.float32)
    for h in range(N_HEADS_B):
        sl = slice(h * HEAD_SLOT, (h + 1) * HEAD_SLOT)
        kb_ref[:, sl] = (kall[:, sl] + kr).astype(bf)
    put_chunks(vbt_ref, lax.dot_general(wukvvt_ref[...], ckvn, _NT,
                                        preferred_element_type=jnp.float32))


def _const_spec(shape):
    return pl.BlockSpec(shape, lambda *_: (0,) * len(shape), pipeline_mode=pl.Buffered(1))


def _proj_call(x2, posc, invl, gmix, win, wvat, wwit, gcq, wuq, gckv, wukvk, wukvvt):
    n_tok = x2.shape[0]
    tm = TM_PROJ
    bf = jnp.bfloat16
    row = lambda w: pl.BlockSpec((tm, w), lambda t: (t, 0))
    col = lambda r: pl.BlockSpec((r, tm), lambda t: (0, t))
    chunks = lambda r: pl.BlockSpec((tm // KC, r, KC), lambda t: (t, 0, 0))
    in_specs = [row(D_MODEL), row(1), _const_spec(invl.shape), _const_spec(gmix.shape),
                _const_spec(win.shape), _const_spec(wvat.shape), _const_spec(wwit.shape),
                _const_spec(gcq.shape), _const_spec(wuq.shape), _const_spec(gckv.shape),
                _const_spec(wukvk.shape), _const_spec(wukvvt.shape)]
    out_shape = [
        jax.ShapeDtypeStruct((n_tok, C_QI - C_QA), bf),
        jax.ShapeDtypeStruct((n_tok, C_KA - C_QI), bf),
        jax.ShapeDtypeStruct((n_tok, N_HEADS_B * HEAD_SLOT), bf),
        jax.ShapeDtypeStruct((SUBLANE, n_tok), jnp.float32),
        jax.ShapeDtypeStruct((n_tok, KA_PAD), bf),
        jax.ShapeDtypeStruct((n_tok, KI_PAD), bf),
        jax.ShapeDtypeStruct((n_tok, N_HEADS_B * HEAD_SLOT), bf),
        jax.ShapeDtypeStruct((n_tok // KC, N_KV_HEADS_A * V_ROWS, KC), bf),
        jax.ShapeDtypeStruct((n_tok // KC, N_HEADS_B * V_ROWS, KC), bf),
        jax.ShapeDtypeStruct((n_tok, D_MODEL), bf),
        jax.ShapeDtypeStruct((n_tok, D_MODEL), bf),
        jax.ShapeDtypeStruct((n_tok, LANE), jnp.float32),
    ]
    out_specs = [row(C_QI - C_QA), row(C_KA - C_QI), row(N_HEADS_B * HEAD_SLOT), col(SUBLANE),
                 row(KA_PAD), row(KI_PAD), row(N_HEADS_B * HEAD_SLOT),
                 chunks(N_KV_HEADS_A * V_ROWS), chunks(N_HEADS_B * V_ROWS),
                 row(D_MODEL), row(D_MODEL), row(LANE)]
    return pl.pallas_call(
        _proj_kernel,
        grid=(n_tok // tm,),
        in_specs=in_specs,
        out_specs=out_specs,
        out_shape=out_shape,
        compiler_params=pltpu.CompilerParams(
            dimension_semantics=("arbitrary",), vmem_limit_bytes=VMEM_LIMIT),
        name="proj",
    )(x2, posc, invl, gmix, win, wvat, wwit, gcq, wuq, gckv, wukvk, wukvvt)


def _col_sum(a):
    r = a.shape[0]
    part = jnp.sum(a.reshape(r // SUBLANE, SUBLANE, a.shape[1]), axis=0)
    return jnp.sum(part, axis=0, keepdims=True)


def _col_max(a):
    r = a.shape[0]
    part = jnp.max(a.reshape(r // SUBLANE, SUBLANE, a.shape[1]), axis=0)
    return jnp.max(part, axis=0, keepdims=True)


def _attn_kernel(qa_ref, qi_ref, qb_ref, wit_ref, posq_ref,
                 kap_ref, kip_ref, kb_ref, vat_ref, vbt_ref, posk_ref,
                 yat_ref, ybt_ref,
                 score_ref, sb_ref, lg_ref, acc_ref, m_ref, mu_ref, *, topk):
    bf = jnp.bfloat16
    f32 = jnp.float32
    i = pl.program_id(1)
    nch = i + 1
    row_i = lax.broadcasted_iota(jnp.int32, (KC, TQ), 0)
    col_i = lax.broadcasted_iota(jnp.int32, (KC, TQ), 1)
    causal = row_i <= col_i
    kf = float(topk)

    def kslice(c):
        if isinstance(c, int):
            return pl.ds(c * KC, KC)
        return pl.ds(pl.multiple_of(c * KC, KC), KC)

    w = wit_ref[...]

    def score_chunk(c, carry):
        acc = jnp.zeros((KC, TQ), f32)
        for h in range(N_IDX_HEADS):
            kk = kip_ref[kslice(c), (h % 2) * LANE:(h % 2 + 1) * LANE]
            qq = qi_ref[:, (h // 2) * LANE:(h // 2 + 1) * LANE]
            d = lax.dot_general(kk, qq, _NT, preferred_element_type=f32)
            acc = acc + jnp.maximum(d, 0.0) * w[h:h + 1, :]
        score_ref[c] = acc
        sb_ref[c] = acc.astype(bf)
        return carry

    lax.fori_loop(0, nch, score_chunk, 0)
    diag = jnp.where(causal, score_ref[i], -jnp.inf)
    score_ref[i] = diag
    sb_ref[i] = diag.astype(bf)

    def count_rounded(thr_b):
        def body(c, acc):
            hit = (sb_ref[c] >= thr_b).reshape(KC // COUNT_ROWS, COUNT_ROWS, TQ)
            for r in range(KC // COUNT_ROWS):
                acc = jnp.where(hit[r], acc + 1.0, acc)
            return acc
        acc = lax.fori_loop(0, nch, body, jnp.zeros((COUNT_ROWS, TQ), bf))
        return _col_sum(acc.astype(f32))

    def key16_to_bf16(key):
        bits = jnp.where(key >= 0, key, key ^ jnp.int32(0x7FFF))
        return lax.bitcast_convert_type(lax.shift_left(bits, jnp.int32(16)), f32).astype(bf)

    def coarse_step(j, key):
        cand = key + lax.shift_left(jnp.int32(1), jnp.int32(15) - j)
        cnt = count_rounded(key16_to_bf16(cand))
        return jnp.where(cnt >= kf, cand, key)

    key16 = lax.fori_loop(0, 16, coarse_step, jnp.full((1, TQ), -(1 << 15), jnp.int32))
    fine_lo = lax.shift_left(key16 - jnp.where(key16 >= 0, 1, 0), jnp.int32(16))

    def count(pred):
        def body(c, acc):
            hit = pred(score_ref[c]).reshape(KC // COUNT_ROWS, COUNT_ROWS, TQ)
            for r in range(KC // COUNT_ROWS):
                acc = jnp.where(hit[r], acc + 1.0, acc)
            return acc
        acc = lax.fori_loop(0, nch, body, jnp.zeros((COUNT_ROWS, TQ), f32))
        return _col_sum(acc)

    def key_to_float(key):
        bits = jnp.where(key >= 0, key, key ^ jnp.int32(0x7FFFFFFF))
        return lax.bitcast_convert_type(bits, f32)

    def fine_step(j, off):
        cand = off + lax.shift_left(jnp.int32(1), jnp.int32(FINE_BITS - 1) - j)
        thr_c = key_to_float(fine_lo + cand)
        cnt = count(lambda s: s >= thr_c)
        return jnp.where(cnt >= kf, cand, off)

    key = fine_lo + lax.fori_loop(0, FINE_BITS, fine_step, jnp.zeros((1, TQ), jnp.int32))
    t_glob = i * TQ + lax.broadcasted_iota(jnp.int32, (1, TQ), 1)
    take_all = t_glob < topk
    thr = jnp.where(take_all, jnp.finfo(f32).min, key_to_float(key))
    n_ge = count(lambda s: s >= thr)
    surplus = jnp.where(take_all, 0.0, n_ge - kf)
    posq = posq_ref[...]

    any_surplus = jnp.max(surplus) > 0.0

    def distance(c):
        pk = posk_ref[kslice(c), :]
        return jnp.abs(jnp.concatenate([pk] * (TQ // LANE), axis=1) - posq)

    @pl.when(jnp.logical_not(any_surplus))
    def _select_all_ties():
        def chunk(c, carry):
            score_ref[c] = jnp.where(score_ref[c] >= thr, distance(c), FAR)
            return carry

        lax.fori_loop(0, nch, chunk, 0)

    @pl.when(any_surplus)
    def _select_ranked_ties():
        n_gt = count(lambda s: s > thr)
        need = jnp.where(take_all, float(score_ref.shape[0] * KC + 1), kf - n_gt)
        tri = jnp.where(lax.broadcasted_iota(jnp.int32, (KC, KC), 0)
                        >= lax.broadcasted_iota(jnp.int32, (KC, KC), 1), 1.0, 0.0).astype(bf)

        def chunk(c, taken):
            s = score_ref[c]
            eq = s == thr
            pc = jnp.dot(tri, jnp.where(eq, 1.0, 0.0).astype(bf),
                         preferred_element_type=f32) + taken
            rank = jnp.where(eq, pc, jnp.where(s > thr, -jnp.inf, jnp.inf))
            score_ref[c] = jnp.where(rank <= need, distance(c), FAR)
            return pc[KC - 1:KC, :]

        lax.fori_loop(0, nch, chunk, jnp.zeros((1, TQ), f32))

    def dsa_logits(c, h, first):
        g = h // GROUP_A
        kk = kap_ref[kslice(c), (2 * g + h % 2) * LANE:(2 * g + h % 2 + 1) * LANE]
        qq = qa_ref[:, (h // 2) * LANE:(h // 2 + 1) * LANE]
        slope = 2.0 ** (-8.0 * (h + 1) / N_HEADS_A) * LOG2E
        qk = lax.dot_general(kk, qq, _NT, preferred_element_type=f32)
        return qk - slope * score_ref[c]

    def dsa_values(c, h):
        g = h // GROUP_A
        return vat_ref[c, g * V_ROWS:(g + 1) * V_ROWS, :]

    def mla_chunk_of(j):
        return jnp.where(j == 0, i, j - 1)

    def mla_logits(c, h, first):
        kk = kb_ref[kslice(c), h * HEAD_SLOT:(h + 1) * HEAD_SLOT]
        qq = qb_ref[:, h * HEAD_SLOT:(h + 1) * HEAD_SLOT]
        qk = lax.dot_general(kk, qq, _NT, preferred_element_type=f32)
        return jnp.where(causal, qk, NEG) if first else qk

    def mla_values(c, h):
        return vbt_ref[c, h * V_ROWS:(h + 1) * V_ROWS, :]

    n_slots = N_HEADS_A + N_HEADS_B
    m_ref[...] = jnp.full(m_ref.shape, NEG, f32)
    mu_ref[...] = jnp.full(mu_ref.shape, NEG, f32)
    acc_ref[...] = jnp.zeros(acc_ref.shape, f32)

    def slot_logits(j, s, first):
        if s < N_HEADS_A:
            return dsa_logits(j, s, first)
        return mla_logits(mla_chunk_of(j), s - N_HEADS_A, first)

    def slot_values(j, s):
        if s < N_HEADS_A:
            return dsa_values(j, s)
        return mla_values(mla_chunk_of(j), s - N_HEADS_A)

    def stage(j, s, first):
        lg = slot_logits(j, s, first)
        lg_ref[s] = lg
        m_ref[s:s + 1, :] = jnp.maximum(m_ref[s:s + 1, :], _col_max(lg))

    def consume(j, s):
        m_now = m_ref[s:s + 1, :]
        alpha = jnp.exp2(mu_ref[s:s + 1, :] - m_now)
        mu_ref[s:s + 1, :] = m_now
        p = jnp.exp2(lg_ref[s] - m_now)
        pv = jnp.dot(slot_values(j, s), p.astype(bf), preferred_element_type=f32)
        acc_ref[s] = alpha * acc_ref[s] + pv

    slot_order = [s for pair in zip(range(N_HEADS_A), range(N_HEADS_A, n_slots)) for s in pair]
    for s in slot_order:
        stage(0, s, True)

    def sweep(j, carry):
        for s in slot_order:
            consume(j, s)
            stage(j + 1, s, False)
        return carry

    lax.fori_loop(0, i, sweep, 0)
    for s in range(n_slots):
        consume(i, s)
        out_ref, h, dv = (yat_ref, s, HEAD_DIM_A) if s < N_HEADS_A else (ybt_ref, s - N_HEADS_A, V_DIM_B)
        inv_l = 1.0 / acc_ref[s, dv:dv + 1, :]
        out_ref[h * dv:(h + 1) * dv, :] = (acc_ref[s, :dv, :] * inv_l).astype(out_ref.dtype)


def _attn_call(qa, qi, qb, wit, posr, kap, kip, kb, vat, vbt, posc, batch, seq, topk):
    nq = seq // TQ
    bf = jnp.bfloat16
    n_tok = batch * seq
    n_slots = N_HEADS_A + N_HEADS_B
    assert HEAD_DIM_A == V_DIM_B
    qrow = lambda w: pl.BlockSpec((TQ, w), lambda b, i: (b * nq + i, 0))
    qcol = lambda r: pl.BlockSpec((r, TQ), lambda b, i: (0, b * nq + i))
    krow = lambda w: pl.BlockSpec((seq, w), lambda b, i: (b, 0))
    kchunks = lambda r: pl.BlockSpec((seq // KC, r, KC), lambda b, i: (b, 0, 0))
    in_specs = [qrow(qa.shape[1]), qrow(qi.shape[1]), qrow(qb.shape[1]), qcol(SUBLANE), qcol(1),
                krow(kap.shape[1]), krow(kip.shape[1]), krow(kb.shape[1]),
                kchunks(vat.shape[1]), kchunks(vbt.shape[1]), krow(LANE)]
    out_shape = [jax.ShapeDtypeStruct((N_HEADS_A * HEAD_DIM_A, n_tok), bf),
                 jax.ShapeDtypeStruct((N_HEADS_B * V_DIM_B, n_tok), bf)]
    out_specs = [qcol(N_HEADS_A * HEAD_DIM_A), qcol(N_HEADS_B * V_DIM_B)]
    scratch = [pltpu.VMEM((seq // KC, KC, TQ), jnp.float32),
               pltpu.VMEM((seq // KC, KC, TQ), bf),
               pltpu.VMEM((n_slots, KC, TQ), jnp.float32),
               pltpu.VMEM((n_slots, V_ROWS, TQ), jnp.float32),
               pltpu.VMEM((n_slots, TQ), jnp.float32),
               pltpu.VMEM((n_slots, TQ), jnp.float32)]
    return pl.pallas_call(
        functools.partial(_attn_kernel, topk=topk),
        grid=(batch, nq),
        in_specs=in_specs,
        out_specs=out_specs,
        out_shape=out_shape,
        scratch_shapes=scratch,
        compiler_params=pltpu.CompilerParams(
            dimension_semantics=("arbitrary", "arbitrary"), vmem_limit_bytes=VMEM_LIMIT),
        name="attn",
    )(qa, qi, qb, wit, posr, kap, kip, kb, vat, vbt, posc)


def _post_kernel(x_ref, yat_ref, ybt_ref, sga_ref, sgb_ref, p_ref,
                 wpa_ref, wpb_ref, wout_ref, gmlp_ref, wff1_ref, wff2_ref, gple_ref,
                 wpg_ref, wple_ref, gfin_ref, o_ref, *, final_norm):
    bf = jnp.bfloat16
    f32 = jnp.float32
    a = lax.dot_general(yat_ref[...], wpa_ref[...], _TN, preferred_element_type=f32)
    b = lax.dot_general(ybt_ref[...], wpb_ref[...], _TN, preferred_element_type=f32)
    merged = sga_ref[...].astype(f32) * a + sgb_ref[...].astype(f32) * b
    h = x_ref[...] + jnp.dot(merged.astype(bf), wout_ref[...], preferred_element_type=f32)
    n2 = _rms(h, gmlp_ref[...]).astype(bf)
    ff = jnp.zeros_like(h)
    for c in range(D_FF // FF_CHUNK):
        sl = slice(c * FF_CHUNK, (c + 1) * FF_CHUNK)
        u = jnp.maximum(jnp.dot(n2, wff1_ref[:, sl], preferred_element_type=f32), 0.0)
        ff = ff + jnp.dot((u * u).astype(bf), wff2_ref[sl, :], preferred_element_type=f32)
    h = h + ff
    n3 = _rms(h, gple_ref[...]).astype(bf)
    gate = jax.nn.sigmoid(jnp.dot(n3, wpg_ref[...], preferred_element_type=f32))
    pe = jnp.dot(p_ref[...].astype(bf), wple_ref[...], preferred_element_type=f32)
    h = h + gate * pe
    o_ref[...] = _rms(h, gfin_ref[...]) if final_norm else h


def _post_call(x2, yat, ybt, sga, sgb, p2, wpa, wpb, wout, gmlp, wff1, wff2, gple, wpg, wple, gfin,
               final_norm):
    n_tok = x2.shape[0]
    tm = TM_POST
    row = lambda w: pl.BlockSpec((tm, w), lambda t: (t, 0))
    col = lambda r: pl.BlockSpec((r, tm), lambda t: (0, t))
    wspec = lambda a: pl.BlockSpec(a.shape, lambda t: (0,) * a.ndim, pipeline_mode=pl.Buffered(1))
    in_specs = [row(D_MODEL), col(yat.shape[0]), col(ybt.shape[0]), row(D_MODEL), row(D_MODEL),
                row(PLE_DIM)] + [wspec(a) for a in
                                 (wpa, wpb, wout, gmlp, wff1, wff2, gple, wpg, wple, gfin)]
    return pl.pallas_call(
        functools.partial(_post_kernel, final_norm=final_norm),
        grid=(n_tok // tm,),
        in_specs=in_specs,
        out_specs=row(D_MODEL),
        out_shape=jax.ShapeDtypeStruct((n_tok, D_MODEL), jnp.float32),
        compiler_params=pltpu.CompilerParams(
            dimension_semantics=("arbitrary",), vmem_limit_bytes=VMEM_LIMIT),
        name="post",
    )(x2, yat, ybt, sga, sgb, p2, wpa, wpb, wout, gmlp, wff1, wff2, gple, wpg, wple, gfin)


def _pack_layer(w_in, w_uq, w_ukv):
    bf = jnp.bfloat16
    sizes = (N_HEADS_A * HEAD_DIM_A, N_KV_HEADS_A * HEAD_DIM_A, N_KV_HEADS_A * HEAD_DIM_A,
             N_IDX_HEADS * IDX_DIM, IDX_DIM, N_IDX_HEADS, Q_LORA, KV_LORA, ROPE_DIM,
             D_MODEL, D_MODEL)
    offs = np.concatenate([[0], np.cumsum(sizes)])
    wqa, wka, wva, wqi, wki, wwi, wcq, wckv, wkr, wga, wgb = [
        w_in[:, offs[j]:offs[j + 1]] for j in range(len(sizes))]
    d = w_in.shape[0]
    small_pad = jnp.zeros((d, LANE - IDX_DIM - ROPE_DIM), w_in.dtype)
    win = jnp.concatenate([wqa, wqi, wka, wki, wkr, small_pad, wcq, wckv, wga, wgb],
                          axis=1).astype(bf)
    wvat = wva.T.astype(bf)
    wwit = jnp.concatenate([wwi.T, jnp.zeros((SUBLANE - N_IDX_HEADS, d), w_in.dtype)],
                           axis=0).astype(bf)
    uq = w_uq.reshape(Q_LORA, N_HEADS_B, NOPE_DIM + ROPE_DIM)
    uq = jnp.pad(uq, ((0, 0), (0, 0), (0, HEAD_SLOT - NOPE_DIM - ROPE_DIM)))
    wuq = uq.reshape(Q_LORA, N_HEADS_B * HEAD_SLOT).astype(bf)
    ukv = w_ukv.reshape(KV_LORA, N_HEADS_B, NOPE_DIM + V_DIM_B)
    uk = jnp.pad(ukv[:, :, :NOPE_DIM], ((0, 0), (0, 0), (0, HEAD_SLOT - NOPE_DIM)))
    wukvk = uk.reshape(KV_LORA, N_HEADS_B * HEAD_SLOT).astype(bf)
    wukvvt = ukv[:, :, NOPE_DIM:].reshape(KV_LORA, N_HEADS_B * V_DIM_B).T.astype(bf)
    return win, wvat, wwit, wuq, wukvk, wukvvt


def kernel(x, p, positions, g_mix, w_in, g_cq, w_uq, g_ckv, w_ukv, w_pa, w_pb, w_out, g_mlp,
           w_ff1, w_ff2, g_ple, w_ple_gate, w_ple, g_final):
    bf = jnp.bfloat16
    batch, seq, d = x.shape
    depth = w_in.shape[0]
    n_tok = batch * seq
    topk = min(TOPK_MAX, seq // 4)
    assert d == D_MODEL and seq % TQ == 0 and TQ == KC and topk <= TQ and n_tok % TM_PROJ == 0
    assert IDX_DIM == NOPE_DIM == HEAD_DIM_A == LANE // 2

    posf = positions.astype(jnp.float32)
    posc = posf.reshape(n_tok, 1)
    posr = posf.reshape(1, n_tok)
    inv = ROPE_THETA ** (-jnp.arange(HALF_ROPE, dtype=jnp.float32) / HALF_ROPE)
    invl = jnp.concatenate([jnp.zeros((NOPE_DIM,), jnp.float32), inv, inv,
                            jnp.zeros((HEAD_SLOT - NOPE_DIM - ROPE_DIM,), jnp.float32)]).reshape(1, LANE)

    h = x.reshape(n_tok, d)
    for li in range(depth):
        win, wvat, wwit, wuq, wukvk, wukvvt = _pack_layer(w_in[li], w_uq[li], w_ukv[li])
        (qa, qi, qb, wit, kap, kip, kb, vat, vbt, sga, sgb, posb) = _proj_call(
            h, posc, invl, g_mix[li].reshape(1, d), win, wvat, wwit, g_cq[li].reshape(1, -1),
            wuq, g_ckv[li].reshape(1, -1), wukvk, wukvvt)
        yat, ybt = _attn_call(qa, qi, qb, wit, posr, kap, kip, kb, vat, vbt, posb, batch, seq, topk)
        h = _post_call(h, yat, ybt, sga, sgb, p[li].reshape(n_tok, -1),
                       w_pa[li].astype(bf), w_pb[li].astype(bf), w_out[li].astype(bf),
                       g_mlp[li].reshape(1, d), w_ff1[li].astype(bf), w_ff2[li].astype(bf),
                       g_ple[li].reshape(1, d), w_ple_gate[li].astype(bf), w_ple[li].astype(bf),
                       g_final.reshape(1, d), li == depth - 1)
    return h.reshape(batch, seq, d)
```

```python
import functools

import jax
import jax.numpy as jnp
import numpy as np
from jax import lax
from jax.experimental import pallas as pl
from jax.experimental.pallas import tpu as pltpu

D_MODEL = 1024
N_HEADS_A = 8
N_KV_HEADS_A = 2
HEAD_DIM_A = 64
GROUP_A = N_HEADS_A // N_KV_HEADS_A
N_IDX_HEADS = 4
IDX_DIM = 64
TOPK_MAX = 256
N_HEADS_B = 8
Q_LORA = 384
KV_LORA = 256
NOPE_DIM = 64
ROPE_DIM = 32
HALF_ROPE = ROPE_DIM // 2
V_DIM_B = 64
ROPE_THETA = 10000.0
D_FF = 4 * D_MODEL
PLE_DIM = 256
EPS = 1e-6

LANE = 128
SUBLANE = 8
HEAD_SLOT = 128
V_ROWS = V_DIM_B + 2 * SUBLANE
VMEM_LIMIT = 56 * 1024 * 1024

TM_PROJ = 1024
TM_POST = 512
TQ = 256
KC = 256
FF_CHUNK = 1024
NEG = -1e30
FAR = 1e30
LOG2E = 1.4426950408889634
COUNT_ROWS = 32
FINE_BITS = 17

C_QA = 0
C_QI = C_QA + N_HEADS_A * HEAD_DIM_A
C_KA = C_QI + N_IDX_HEADS * IDX_DIM
C_SM = C_KA + N_KV_HEADS_A * HEAD_DIM_A
C_CQ = C_SM + LANE
C_CKV = C_CQ + Q_LORA
C_GA = C_CKV + KV_LORA
C_GB = C_GA + D_MODEL
C_END = C_GB + D_MODEL
KA_PAD = 2 * N_KV_HEADS_A * LANE
KI_PAD = 2 * LANE

_NT = (((1,), (1,)), ((), ()))
_TN = (((0,), (0,)), ((), ()))


def _rms(x, g):
    return x * lax.rsqrt(jnp.mean(x * x, axis=-1, keepdims=True) + EPS) * g


def _rope_slot(x, cos_t, sin_dn, sin_up):
    up = pltpu.roll(x, HALF_ROPE, 1)
    dn = pltpu.roll(x, LANE - HALF_ROPE, 1)
    return x * cos_t + dn * sin_dn + up * sin_up


def _proj_kernel(x_ref, pos_ref, invl_ref, gmix_ref, win_ref, wvat_ref, wwit_ref, gcq_ref,
                 wuq_ref, gckv_ref, wukvk_ref, wukvvt_ref,
                 qa_ref, qi_ref, qb_ref, wit_ref, kap_ref, kip_ref, kb_ref, vat_ref, vbt_ref,
                 sga_ref, sgb_ref, posb_ref):
    bf = jnp.bfloat16
    n = _rms(x_ref[...], gmix_ref[...]).astype(bf)
    posb_ref[...] = jnp.broadcast_to(pos_ref[...], posb_ref.shape)

    def cols(c0, c1):
        return jnp.dot(n, win_ref[:, c0:c1], preferred_element_type=jnp.float32)

    cqn = _rms(cols(C_CQ, C_CKV), gcq_ref[...]).astype(bf)
    ckvn = _rms(cols(C_CKV, C_GA), gckv_ref[...]).astype(bf)

    lane = lax.broadcasted_iota(jnp.int32, (1, LANE), 1)
    is_x1 = (lane >= NOPE_DIM) & (lane < NOPE_DIM + HALF_ROPE)
    is_x2 = (lane >= NOPE_DIM + HALF_ROPE) & (lane < NOPE_DIM + ROPE_DIM)
    ang = pos_ref[...] * invl_ref[...]
    cos, sin = jnp.cos(ang), jnp.sin(ang)
    cos_t = jnp.where(is_x1 | is_x2, cos, 1.0)
    sin_dn = jnp.where(is_x1, -sin, 0.0)
    sin_up = jnp.where(is_x2, sin, 0.0)

    qa_ref[...] = (cols(C_QA, C_QI) * (HEAD_DIM_A ** -0.5 * LOG2E)).astype(bf)
    qi_ref[...] = cols(C_QI, C_KA).astype(bf)
    low = lane < LANE // 2

    def half_slabs(z):
        swapped = pltpu.roll(z, LANE // 2, 1)
        return (jnp.where(low, z, 0.0), jnp.where(low, 0.0, swapped),
                jnp.where(low, swapped, 0.0), jnp.where(low, 0.0, z))

    for j, slab in enumerate(half_slabs(cols(C_KA, C_SM))):
        kap_ref[:, j * LANE:(j + 1) * LANE] = slab.astype(bf)
    small = cols(C_SM, C_CQ)
    ki_slabs = half_slabs(small)
    kip_ref[:, :LANE] = ki_slabs[0].astype(bf)
    kip_ref[:, LANE:] = ki_slabs[1].astype(bf)
    sga_ref[...] = jax.nn.sigmoid(cols(C_GA, C_GB)).astype(bf)
    sgb_ref[...] = jax.nn.sigmoid(cols(C_GB, C_END)).astype(bf)
    def put_chunks(ref, vt):
        ones = jnp.ones((V_ROWS - V_DIM_B, KC), bf)
        for j in range(TM_PROJ // KC):
            for g in range(vt.shape[0] // V_DIM_B):
                ref[j, g * V_ROWS:g * V_ROWS + V_DIM_B, :] = (
                    vt[g * V_DIM_B:(g + 1) * V_DIM_B, j * KC:(j + 1) * KC].astype(bf))
                ref[j, g * V_ROWS + V_DIM_B:(g + 1) * V_ROWS, :] = ones

    put_chunks(vat_ref, lax.dot_general(wvat_ref[...], n, _NT, preferred_element_type=jnp.float32))
    wit_ref[...] = lax.dot_general(wwit_ref[...], n, _NT,
                                   preferred_element_type=jnp.float32) * ((N_IDX_HEADS * IDX_DIM) ** -0.5)

    qscale = (NOPE_DIM + ROPE_DIM) ** -0.5 * LOG2E
    qall = jnp.dot(cqn, wuq_ref[...], preferred_element_type=jnp.float32)
    for h in range(N_HEADS_B):
        sl = slice(h * HEAD_SLOT, (h + 1) * HEAD_SLOT)
        qb_ref[:, sl] = (_rope_slot(qall[:, sl], cos_t, sin_dn, sin_up) * qscale).astype(bf)

    kr = _rope_slot(jnp.where(low, 0.0, small), cos_t, sin_dn, sin_up)
    kall = jnp.dot(ckvn, wukvk_ref[...], preferred_element_type=jnp.float32)
    for h in range(N_HEADS_B):
        sl = slice(h * HEAD_SLOT, (h + 1) * HEAD_SLOT)
        kb_ref[:, sl] = (kall[:, sl] + kr).astype(bf)
    put_chunks(vbt_ref, lax.dot_general(wukvvt_ref[...], ckvn, _NT,
                                        preferred_element_type=jnp.float32))


def _const_spec(shape):
    return pl.BlockSpec(shape, lambda *_: (0,) * len(shape), pipeline_mode=pl.Buffered(1))


def _proj_call(x2, posc, invl, gmix, win, wvat, wwit, gcq, wuq, gckv, wukvk, wukvvt):
    n_tok = x2.shape[0]
    tm = TM_PROJ
    bf = jnp.bfloat16
    row = lambda w: pl.BlockSpec((tm, w), lambda t: (t, 0))
    col = lambda r: pl.BlockSpec((r, tm), lambda t: (0, t))
    chunks = lambda r: pl.BlockSpec((tm // KC, r, KC), lambda t: (t, 0, 0))
    in_specs = [row(D_MODEL), row(1), _const_spec(invl.shape), _const_spec(gmix.shape),
                _const_spec(win.shape), _const_spec(wvat.shape), _const_spec(wwit.shape),
                _const_spec(gcq.shape), _const_spec(wuq.shape), _const_spec(gckv.shape),
                _const_spec(wukvk.shape), _const_spec(wukvvt.shape)]
    out_shape = [
        jax.ShapeDtypeStruct((n_tok, C_QI - C_QA), bf),
        jax.ShapeDtypeStruct((n_tok, C_KA - C_QI), bf),
        jax.ShapeDtypeStruct((n_tok, N_HEADS_B * HEAD_SLOT), bf),
        jax.ShapeDtypeStruct((SUBLANE, n_tok), jnp.float32),
        jax.ShapeDtypeStruct((n_tok, KA_PAD), bf),
        jax.ShapeDtypeStruct((n_tok, KI_PAD), bf),
        jax.ShapeDtypeStruct((n_tok, N_HEADS_B * HEAD_SLOT), bf),
        jax.ShapeDtypeStruct((n_tok // KC, N_KV_HEADS_A * V_ROWS, KC), bf),
        jax.ShapeDtypeStruct((n_tok // KC, N_HEADS_B * V_ROWS, KC), bf),
        jax.ShapeDtypeStruct((n_tok, D_MODEL), bf),
        jax.ShapeDtypeStruct((n_tok, D_MODEL), bf),
        jax.ShapeDtypeStruct((n_tok, LANE), jnp.float32),
    ]
    out_specs = [row(C_QI - C_QA), row(C_KA - C_QI), row(N_HEADS_B * HEAD_SLOT), col(SUBLANE),
                 row(KA_PAD), row(KI_PAD), row(N_HEADS_B * HEAD_SLOT),
                 chunks(N_KV_HEADS_A * V_ROWS), chunks(N_HEADS_B * V_ROWS),
                 row(D_MODEL), row(D_MODEL), row(LANE)]
    return pl.pallas_call(
        _proj_kernel,
        grid=(n_tok // tm,),
        in_specs=in_specs,
        out_specs=out_specs,
        out_shape=out_shape,
        compiler_params=pltpu.CompilerParams(
            dimension_semantics=("arbitrary",), vmem_limit_bytes=VMEM_LIMIT),
        name="proj",
    )(x2, posc, invl, gmix, win, wvat, wwit, gcq, wuq, gckv, wukvk, wukvvt)


def _col_sum(a):
    r = a.shape[0]
    part = jnp.sum(a.reshape(r // SUBLANE, SUBLANE, a.shape[1]), axis=0)
    return jnp.sum(part, axis=0, keepdims=True)


def _col_max(a):
    r = a.shape[0]
    part = jnp.max(a.reshape(r // SUBLANE, SUBLANE, a.shape[1]), axis=0)
    return jnp.max(part, axis=0, keepdims=True)


def _attn_kernel(qa_ref, qi_ref, qb_ref, wit_ref, posq_ref,
                 kap_ref, kip_ref, kb_ref, vat_ref, vbt_ref, posk_ref,
                 yat_ref, ybt_ref,
                 score_ref, sb_ref, lg_ref, acc_ref, m_ref, mu_ref, *, topk):
    bf = jnp.bfloat16
    f32 = jnp.float32
    i = pl.program_id(1)
    nch = i + 1
    row_i = lax.broadcasted_iota(jnp.int32, (KC, TQ), 0)
    col_i = lax.broadcasted_iota(jnp.int32, (KC, TQ), 1)
    causal = row_i <= col_i
    kf = float(topk)

    def kslice(c):
        if isinstance(c, int):
            return pl.ds(c * KC, KC)
        return pl.ds(pl.multiple_of(c * KC, KC), KC)

    w = wit_ref[...]

    def score_chunk(c, carry):
        acc = jnp.zeros((KC, TQ), f32)
        for h in range(N_IDX_HEADS):
            kk = kip_ref[kslice(c), (h % 2) * LANE:(h % 2 + 1) * LANE]
            qq = qi_ref[:, (h // 2) * LANE:(h // 2 + 1) * LANE]
            d = lax.dot_general(kk, qq, _NT, preferred_element_type=f32)
            acc = acc + jnp.maximum(d, 0.0) * w[h:h + 1, :]
        score_ref[c] = acc
        sb_ref[c] = acc.astype(bf)
        return carry

    lax.fori_loop(0, nch, score_chunk, 0)
    diag = jnp.where(causal, score_ref[i], -jnp.inf)
    score_ref[i] = diag
    sb_ref[i] = diag.astype(bf)

    def count_rounded(thr_b):
        def body(c, acc):
            hit = (sb_ref[c] >= thr_b).reshape(KC // COUNT_ROWS, COUNT_ROWS, TQ)
            for r in range(KC // COUNT_ROWS):
                acc = jnp.where(hit[r], acc + 1.0, acc)
            return acc
        acc = lax.fori_loop(0, nch, body, jnp.zeros((COUNT_ROWS, TQ), bf))
        return _col_sum(acc.astype(f32))

    def key16_to_bf16(key):
        bits = jnp.where(key >= 0, key, key ^ jnp.int32(0x7FFF))
        return lax.bitcast_convert_type(lax.shift_left(bits, jnp.int32(16)), f32).astype(bf)

    def coarse_step(j, key):
        cand = key + lax.shift_left(jnp.int32(1), jnp.int32(15) - j)
        cnt = count_rounded(key16_to_bf16(cand))
        return jnp.where(cnt >= kf, cand, key)

    key16 = lax.fori_loop(0, 16, coarse_step, jnp.full((1, TQ), -(1 << 15), jnp.int32))
    fine_lo = lax.shift_left(key16 - jnp.where(key16 >= 0, 1, 0), jnp.int32(16))

    def count(pred):
        def body(c, acc):
            hit = pred(score_ref[c]).reshape(KC // COUNT_ROWS, COUNT_ROWS, TQ)
            for r in range(KC // COUNT_ROWS):
                acc = jnp.where(hit[r], acc + 1.0, acc)
            return acc
        acc = lax.fori_loop(0, nch, body, jnp.zeros((COUNT_ROWS, TQ), f32))
        return _col_sum(acc)

    def key_to_float(key):
        bits = jnp.where(key >= 0, key, key ^ jnp.int32(0x7FFFFFFF))
        return lax.bitcast_convert_type(bits, f32)

    def fine_step(j, off):
        cand = off + lax.shift_left(jnp.int32(1), jnp.int32(FINE_BITS - 1) - j)
        thr_c = key_to_float(fine_lo + cand)
        cnt = count(lambda s: s >= thr_c)
        return jnp.where(cnt >= kf, cand, off)

    key = fine_lo + lax.fori_loop(0, FINE_BITS, fine_step, jnp.zeros((1, TQ), jnp.int32))
    t_glob = i * TQ + lax.broadcasted_iota(jnp.int32, (1, TQ), 1)
    take_all = t_glob < topk
    thr = jnp.where(take_all, jnp.finfo(f32).min, key_to_float(key))
    n_ge = count(lambda s: s >= thr)
    surplus = jnp.where(take_all, 0.0, n_ge - kf)
    posq = posq_ref[...]

    any_surplus = jnp.max(surplus) > 0.0

    def distance(c):
        pk = posk_ref[kslice(c), :]
        return jnp.abs(jnp.concatenate([pk] * (TQ // LANE), axis=1) - posq)

    @pl.when(jnp.logical_not(any_surplus))
    def _select_all_ties():
        def chunk(c, carry):
            score_ref[c] = jnp.where(score_ref[c] >= thr, distance(c), FAR)
            return carry

        lax.fori_loop(0, nch, chunk, 0)

    @pl.when(any_surplus)
    def _select_ranked_ties():
        n_gt = count(lambda s: s > thr)
        need = jnp.where(take_all, float(score_ref.shape[0] * KC + 1), kf - n_gt)
        tri = jnp.where(lax.broadcasted_iota(jnp.int32, (KC, KC), 0)
                        >= lax.broadcasted_iota(jnp.int32, (KC, KC), 1), 1.0, 0.0).astype(bf)

        def chunk(c, taken):
            s = score_ref[c]
            eq = s == thr
            pc = jnp.dot(tri, jnp.where(eq, 1.0, 0.0).astype(bf),
                         preferred_element_type=f32) + taken
            rank = jnp.where(eq, pc, jnp.where(s > thr, -jnp.inf, jnp.inf))
            score_ref[c] = jnp.where(rank <= need, distance(c), FAR)
            return pc[KC - 1:KC, :]

        lax.fori_loop(0, nch, chunk, jnp.zeros((1, TQ), f32))

    def dsa_logits(c, h, first):
        g = h // GROUP_A
        kk = kap_ref[kslice(c), (2 * g + h % 2) * LANE:(2 * g + h % 2 + 1) * LANE]
        qq = qa_ref[:, (h // 2) * LANE:(h // 2 + 1) * LANE]
        slope = 2.0 ** (-8.0 * (h + 1) / N_HEADS_A) * LOG2E
        qk = lax.dot_general(kk, qq, _NT, preferred_element_type=f32)
        return qk - slope * score_ref[c]

    def dsa_values(c, h):
        g = h // GROUP_A
        return vat_ref[c, g * V_ROWS:(g + 1) * V_ROWS, :]

    def mla_chunk_of(j):
        return jnp.where(j == 0, i, j - 1)

    def mla_logits(c, h, first):
        kk = kb_ref[kslice(c), h * HEAD_SLOT:(h + 1) * HEAD_SLOT]
        qq = qb_ref[:, h * HEAD_SLOT:(h + 1) * HEAD_SLOT]
        qk = lax.dot_general(kk, qq, _NT, preferred_element_type=f32)
        return jnp.where(causal, qk, NEG) if first else qk

    def mla_values(c, h):
        return vbt_ref[c, h * V_ROWS:(h + 1) * V_ROWS, :]

    n_slots = N_HEADS_A + N_HEADS_B
    m_ref[...] = jnp.full(m_ref.shape, NEG, f32)
    mu_ref[...] = jnp.full(mu_ref.shape, NEG, f32)
    acc_ref[...] = jnp.zeros(acc_ref.shape, f32)

    def slot_logits(j, s, first):
        if s < N_HEADS_A:
            return dsa_logits(j, s, first)
        return mla_logits(mla_chunk_of(j), s - N_HEADS_A, first)

    def slot_values(j, s):
        if s < N_HEADS_A:
            return dsa_values(j, s)
        return mla_values(mla_chunk_of(j), s - N_HEADS_A)

    def stage(j, s, first):
        lg = slot_logits(j, s, first)
        lg_ref[s] = lg
        m_ref[s:s + 1, :] = jnp.maximum(m_ref[s:s + 1, :], _col_max(lg))

    def consume(j, s):
        m_now = m_ref[s:s + 1, :]
        alpha = jnp.exp2(mu_ref[s:s + 1, :] - m_now)
        mu_ref[s:s + 1, :] = m_now
        p = jnp.exp2(lg_ref[s] - m_now)
        pv = jnp.dot(slot_values(j, s), p.astype(bf), preferred_element_type=f32)
        acc_ref[s] = alpha * acc_ref[s] + pv

    slot_order = [s for pair in zip(range(N_HEADS_A), range(N_HEADS_A, n_slots)) for s in pair]
    for s in slot_order:
        stage(0, s, True)

    def sweep(j, carry):
        for s in slot_order:
            consume(j, s)
            stage(j + 1, s, False)
        return carry

    lax.fori_loop(0, i, sweep, 0)
    for s in range(n_slots):
        consume(i, s)
        out_ref, h, dv = (yat_ref, s, HEAD_DIM_A) if s < N_HEADS_A else (ybt_ref, s - N_HEADS_A, V_DIM_B)
        inv_l = 1.0 / acc_ref[s, dv:dv + 1, :]
        out_ref[h * dv:(h + 1) * dv, :] = (acc_ref[s, :dv, :] * inv_l).astype(out_ref.dtype)


def _attn_call(qa, qi, qb, wit, posr, kap, kip, kb, vat, vbt, posc, batch, seq, topk):
    nq = seq // TQ
    bf = jnp.bfloat16
    n_tok = batch * seq
    n_slots = N_HEADS_A + N_HEADS_B
    assert HEAD_DIM_A == V_DIM_B
    qrow = lambda w: pl.BlockSpec((TQ, w), lambda b, i: (b * nq + i, 0))
    qcol = lambda r: pl.BlockSpec((r, TQ), lambda b, i: (0, b * nq + i))
    krow = lambda w: pl.BlockSpec((seq, w), lambda b, i: (b, 0))
    kchunks = lambda r: pl.BlockSpec((seq // KC, r, KC), lambda b, i: (b, 0, 0))
    in_specs = [qrow(qa.shape[1]), qrow(qi.shape[1]), qrow(qb.shape[1]), qcol(SUBLANE), qcol(1),
                krow(kap.shape[1]), krow(kip.shape[1]), krow(kb.shape[1]),
                kchunks(vat.shape[1]), kchunks(vbt.shape[1]), krow(LANE)]
    out_shape = [jax.ShapeDtypeStruct((N_HEADS_A * HEAD_DIM_A, n_tok), bf),
                 jax.ShapeDtypeStruct((N_HEADS_B * V_DIM_B, n_tok), bf)]
    out_specs = [qcol(N_HEADS_A * HEAD_DIM_A), qcol(N_HEADS_B * V_DIM_B)]
    scratch = [pltpu.VMEM((seq // KC, KC, TQ), jnp.float32),
               pltpu.VMEM((seq // KC, KC, TQ), bf),
               pltpu.VMEM((n_slots, KC, TQ), jnp.float32),
               pltpu.VMEM((n_slots, V_ROWS, TQ), jnp.float32),
               pltpu.VMEM((n_slots, TQ), jnp.float32),
               pltpu.VMEM((n_slots, TQ), jnp.float32)]
    return pl.pallas_call(
        functools.partial(_attn_kernel, topk=topk),
        grid=(batch, nq),
        in_specs=in_specs,
        out_specs=out_specs,
        out_shape=out_shape,
        scratch_shapes=scratch,
        compiler_params=pltpu.CompilerParams(
            dimension_semantics=("arbitrary", "arbitrary"), vmem_limit_bytes=VMEM_LIMIT),
        name="attn",
    )(qa, qi, qb, wit, posr, kap, kip, kb, vat, vbt, posc)


def _post_kernel(x_ref, yat_ref, ybt_ref, sga_ref, sgb_ref, p_ref,
                 wpa_ref, wpb_ref, wout_ref, gmlp_ref, wff1_ref, wff2_ref, gple_ref,
                 wpg_ref, wple_ref, gfin_ref, o_ref, *, final_norm):
    bf = jnp.bfloat16
    f32 = jnp.float32
    a = lax.dot_general(yat_ref[...], wpa_ref[...], _TN, preferred_element_type=f32)
    b = lax.dot_general(ybt_ref[...], wpb_ref[...], _TN, preferred_element_type=f32)
    merged = sga_ref[...].astype(f32) * a + sgb_ref[...].astype(f32) * b
    h = x_ref[...] + jnp.dot(merged.astype(bf), wout_ref[...], preferred_element_type=f32)
    n2 = _rms(h, gmlp_ref[...]).astype(bf)
    ff = jnp.zeros_like(h)
    for c in range(D_FF // FF_CHUNK):
        sl = slice(c * FF_CHUNK, (c + 1) * FF_CHUNK)
        u = jnp.maximum(jnp.dot(n2, wff1_ref[:, sl], preferred_element_type=f32), 0.0)
        ff = ff + jnp.dot((u * u).astype(bf), wff2_ref[sl, :], preferred_element_type=f32)
    h = h + ff
    n3 = _rms(h, gple_ref[...]).astype(bf)
    gate = jax.nn.sigmoid(jnp.dot(n3, wpg_ref[...], preferred_element_type=f32))
    pe = jnp.dot(p_ref[...].astype(bf), wple_ref[...], preferred_element_type=f32)
    h = h + gate * pe
    o_ref[...] = _rms(h, gfin_ref[...]) if final_norm else h


def _post_call(x2, yat, ybt, sga, sgb, p2, wpa, wpb, wout, gmlp, wff1, wff2, gple, wpg, wple, gfin,
               final_norm):
    n_tok = x2.shape[0]
    tm = TM_POST
    row = lambda w: pl.BlockSpec((tm, w), lambda t: (t, 0))
    col = lambda r: pl.BlockSpec((r, tm), lambda t: (0, t))
    wspec = lambda a: pl.BlockSpec(a.shape, lambda t: (0,) * a.ndim, pipeline_mode=pl.Buffered(1))
    in_specs = [row(D_MODEL), col(yat.shape[0]), col(ybt.shape[0]), row(D_MODEL), row(D_MODEL),
                row(PLE_DIM)] + [wspec(a) for a in
                                 (wpa, wpb, wout, gmlp, wff1, wff2, gple, wpg, wple, gfin)]
    return pl.pallas_call(
        functools.partial(_post_kernel, final_norm=final_norm),
        grid=(n_tok // tm,),
        in_specs=in_specs,
        out_specs=row(D_MODEL),
        out_shape=jax.ShapeDtypeStruct((n_tok, D_MODEL), jnp.float32),
        compiler_params=pltpu.CompilerParams(
            dimension_semantics=("arbitrary",), vmem_limit_bytes=VMEM_LIMIT),
        name="post",
    )(x2, yat, ybt, sga, sgb, p2, wpa, wpb, wout, gmlp, wff1, wff2, gple, wpg, wple, gfin)


def _pack_layer(w_in, w_uq, w_ukv):
    bf = jnp.bfloat16
    sizes = (N_HEADS_A * HEAD_DIM_A, N_KV_HEADS_A * HEAD_DIM_A, N_KV_HEADS_A * HEAD_DIM_A,
             N_IDX_HEADS * IDX_DIM, IDX_DIM, N_IDX_HEADS, Q_LORA, KV_LORA, ROPE_DIM,
             D_MODEL, D_MODEL)
    offs = np.concatenate([[0], np.cumsum(sizes)])
    wqa, wka, wva, wqi, wki, wwi, wcq, wckv, wkr, wga, wgb = [
        w_in[:, offs[j]:offs[j + 1]] for j in range(len(sizes))]
    d = w_in.shape[0]
    small_pad = jnp.zeros((d, LANE - IDX_DIM - ROPE_DIM), w_in.dtype)
    win = jnp.concatenate([wqa, wqi, wka, wki, wkr, small_pad, wcq, wckv, wga, wgb],
                          axis=1).astype(bf)
    wvat = wva.T.astype(bf)
    wwit = jnp.concatenate([wwi.T, jnp.zeros((SUBLANE - N_IDX_HEADS, d), w_in.dtype)],
                           axis=0).astype(bf)
    uq = w_uq.reshape(Q_LORA, N_HEADS_B, NOPE_DIM + ROPE_DIM)
    uq = jnp.pad(uq, ((0, 0), (0, 0), (0, HEAD_SLOT - NOPE_DIM - ROPE_DIM)))
    wuq = uq.reshape(Q_LORA, N_HEADS_B * HEAD_SLOT).astype(bf)
    ukv = w_ukv.reshape(KV_LORA, N_HEADS_B, NOPE_DIM + V_DIM_B)
    uk = jnp.pad(ukv[:, :, :NOPE_DIM], ((0, 0), (0, 0), (0, HEAD_SLOT - NOPE_DIM)))
    wukvk = uk.reshape(KV_LORA, N_HEADS_B * HEAD_SLOT).astype(bf)
    wukvvt = ukv[:, :, NOPE_DIM:].reshape(KV_LORA, N_HEADS_B * V_DIM_B).T.astype(bf)
    return win, wvat, wwit, wuq, wukvk, wukvvt


def kernel(x, p, positions, g_mix, w_in, g_cq, w_uq, g_ckv, w_ukv, w_pa, w_pb, w_out, g_mlp,
           w_ff1, w_ff2, g_ple, w_ple_gate, w_ple, g_final):
    bf = jnp.bfloat16
    batch, seq, d = x.shape
    depth = w_in.shape[0]
    n_tok = batch * seq
    topk = min(TOPK_MAX, seq // 4)
    assert d == D_MODEL and seq % TQ == 0 and TQ == KC and topk <= TQ and n_tok % TM_PROJ == 0
    assert IDX_DIM == NOPE_DIM == HEAD_DIM_A == LANE // 2

    posf = positions.astype(jnp.float32)
    posc = posf.reshape(n_tok, 1)
    posr = posf.reshape(1, n_tok)
    inv = ROPE_THETA ** (-jnp.arange(HALF_ROPE, dtype=jnp.float32) / HALF_ROPE)
    invl = jnp.concatenate([jnp.zeros((NOPE_DIM,), jnp.float32), inv, inv,
                            jnp.zeros((HEAD_SLOT - NOPE_DIM - ROPE_DIM,), jnp.float32)]).reshape(1, LANE)

    h = x.reshape(n_tok, d)
    for li in range(depth):
        win, wvat, wwit, wuq, wukvk, wukvvt = _pack_layer(w_in[li], w_uq[li], w_ukv[li])
        (qa, qi, qb, wit, kap, kip, kb, vat, vbt, sga, sgb, posb) = _proj_call(
            h, posc, invl, g_mix[li].reshape(1, d), win, wvat, wwit, g_cq[li].reshape(1, -1),
            wuq, g_ckv[li].reshape(1, -1), wukvk, wukvvt)
        yat, ybt = _attn_call(qa, qi, qb, wit, posr, kap, kip, kb, vat, vbt, posb, batch, seq, topk)
        h = _post_call(h, yat, ybt, sga, sgb, p[li].reshape(n_tok, -1),
                       w_pa[li].astype(bf), w_pb[li].astype(bf), w_out[li].astype(bf),
                       g_mlp[li].reshape(1, d), w_ff1[li].astype(bf), w_ff2[li].astype(bf),
                       g_ple[li].reshape(1, d), w_ple_gate[li].astype(bf), w_ple[li].astype(bf),
                       g_final.reshape(1, d), li == depth - 1)
    return h.reshape(batch, seq, d)
```

```python
import functools

import jax
import jax.numpy as jnp
import numpy as np
from jax import lax
from jax.experimental import pallas as pl
from jax.experimental.pallas import tpu as pltpu

D_MODEL = 1024
N_HEADS_A = 8
N_KV_HEADS_A = 2
HEAD_DIM_A = 64
GROUP_A = N_HEADS_A // N_KV_HEADS_A
N_IDX_HEADS = 4
IDX_DIM = 64
TOPK_MAX = 256
N_HEADS_B = 8
Q_LORA = 384
KV_LORA = 256
NOPE_DIM = 64
ROPE_DIM = 32
HALF_ROPE = ROPE_DIM // 2
V_DIM_B = 64
ROPE_THETA = 10000.0
D_FF = 4 * D_MODEL
PLE_DIM = 256
EPS = 1e-6

LANE = 128
SUBLANE = 8
HEAD_SLOT = 128
V_ROWS = V_DIM_B + 2 * SUBLANE
VMEM_LIMIT = 56 * 1024 * 1024

TM_PROJ = 1024
TM_POST = 512
TQ = 256
KC = 256
FF_CHUNK = 1024
NEG = -1e30
FAR = 1e30
LOG2E = 1.4426950408889634
COUNT_ROWS = 32
FINE_BITS = 17

C_QA = 0
C_QI = C_QA + N_HEADS_A * HEAD_DIM_A
C_KA = C_QI + N_IDX_HEADS * IDX_DIM
C_SM = C_KA + N_KV_HEADS_A * HEAD_DIM_A
C_CQ = C_SM + LANE
C_CKV = C_CQ + Q_LORA
C_GA = C_CKV + KV_LORA
C_GB = C_GA + D_MODEL
C_END = C_GB + D_MODEL
KA_PAD = 2 * N_KV_HEADS_A * LANE
KI_PAD = 2 * LANE

_NT = (((1,), (1,)), ((), ()))
_TN = (((0,), (0,)), ((), ()))


def _rms(x, g):
    return x * lax.rsqrt(jnp.mean(x * x, axis=-1, keepdims=True) + EPS) * g


def _rope_slot(x, cos_t, sin_dn, sin_up):
    up = pltpu.roll(x, HALF_ROPE, 1)
    dn = pltpu.roll(x, LANE - HALF_ROPE, 1)
    return x * cos_t + dn * sin_dn + up * sin_up


def _proj_kernel(x_ref, pos_ref, invl_ref, gmix_ref, win_ref, wvat_ref, wwit_ref, gcq_ref,
                 wuq_ref, gckv_ref, wukvk_ref, wukvvt_ref,
                 qa_ref, qi_ref, qb_ref, wit_ref, kap_ref, kip_ref, kb_ref, vat_ref, vbt_ref,
                 sga_ref, sgb_ref, posb_ref):
    bf = jnp.bfloat16
    n = _rms(x_ref[...], gmix_ref[...]).astype(bf)
    posb_ref[...] = jnp.broadcast_to(pos_ref[...], posb_ref.shape)

    def cols(c0, c1):
        return jnp.dot(n, win_ref[:, c0:c1], preferred_element_type=jnp.float32)

    cqn = _rms(cols(C_CQ, C_CKV), gcq_ref[...]).astype(bf)
    ckvn = _rms(cols(C_CKV, C_GA), gckv_ref[...]).astype(bf)

    lane = lax.broadcasted_iota(jnp.int32, (1, LANE), 1)
    is_x1 = (lane >= NOPE_DIM) & (lane < NOPE_DIM + HALF_ROPE)
    is_x2 = (lane >= NOPE_DIM + HALF_ROPE) & (lane < NOPE_DIM + ROPE_DIM)
    ang = pos_ref[...] * invl_ref[...]
    cos, sin = jnp.cos(ang), jnp.sin(ang)
    cos_t = jnp.where(is_x1 | is_x2, cos, 1.0)
    sin_dn = jnp.where(is_x1, -sin, 0.0)
    sin_up = jnp.where(is_x2, sin, 0.0)

    qa_ref[...] = (cols(C_QA, C_QI) * (HEAD_DIM_A ** -0.5 * LOG2E)).astype(bf)
    qi_ref[...] = cols(C_QI, C_KA).astype(bf)
    low = lane < LANE // 2

    def half_slabs(z):
        swapped = pltpu.roll(z, LANE // 2, 1)
        return (jnp.where(low, z, 0.0), jnp.where(low, 0.0, swapped),
                jnp.where(low, swapped, 0.0), jnp.where(low, 0.0, z))

    for j, slab in enumerate(half_slabs(cols(C_KA, C_SM))):
        kap_ref[:, j * LANE:(j + 1) * LANE] = slab.astype(bf)
    small = cols(C_SM, C_CQ)
    ki_slabs = half_slabs(small)
    kip_ref[:, :LANE] = ki_slabs[0].astype(bf)
    kip_ref[:, LANE:] = ki_slabs[1].astype(bf)
    sga_ref[...] = jax.nn.sigmoid(cols(C_GA, C_GB)).astype(bf)
    sgb_ref[...] = jax.nn.sigmoid(cols(C_GB, C_END)).astype(bf)
    def put_chunks(ref, vt):
        ones = jnp.ones((V_ROWS - V_DIM_B, KC), bf)
        for j in range(TM_PROJ // KC):
            for g in range(vt.shape[0] // V_DIM_B):
                ref[j, g * V_ROWS:g * V_ROWS + V_DIM_B, :] = (
                    vt[g * V_DIM_B:(g + 1) * V_DIM_B, j * KC:(j + 1) * KC].astype(bf))
                ref[j, g * V_ROWS + V_DIM_B:(g + 1) * V_ROWS, :] = ones

    put_chunks(vat_ref, lax.dot_general(wvat_ref[...], n, _NT, preferred_element_type=jnp.float32))
    wit_ref[...] = lax.dot_general(wwit_ref[...], n, _NT,
                                   preferred_element_type=jnp.float32) * ((N_IDX_HEADS * IDX_DIM) ** -0.5)

    qscale = (NOPE_DIM + ROPE_DIM) ** -0.5 * LOG2E
    qall = jnp.dot(cqn, wuq_ref[...], preferred_element_type=jnp.float32)
    for h in range(N_HEADS_B):
        sl = slice(h * HEAD_SLOT, (h + 1) * HEAD_SLOT)
        qb_ref[:, sl] = (_rope_slot(qall[:, sl], cos_t, sin_dn, sin_up) * qscale).astype(bf)

    kr = _rope_slot(jnp.where(low, 0.0, small), cos_t, sin_dn, sin_up)
    kall = jnp.dot(ckvn, wukvk_ref[...], preferred_element_type=jnp.float32)
    for h in range(N_HEADS_B):
        sl = slice(h * HEAD_SLOT, (h + 1) * HEAD_SLOT)
        kb_ref[:, sl] = (kall[:, sl] + kr).astype(bf)
    put_chunks(vbt_ref, lax.dot_general(wukvvt_ref[...], ckvn, _NT,
                                        preferred_element_type=jnp.float32))


def _const_spec(shape):
    return pl.BlockSpec(shape, lambda *_: (0,) * len(shape), pipeline_mode=pl.Buffered(1))


def _proj_call(x2, posc, invl, gmix, win, wvat, wwit, gcq, wuq, gckv, wukvk, wukvvt):
    n_tok = x2.shape[0]
    tm = TM_PROJ
    bf = jnp.bfloat16
    row = lambda w: pl.BlockSpec((tm, w), lambda t: (t, 0))
    col = lambda r: pl.BlockSpec((r, tm), lambda t: (0, t))
    chunks = lambda r: pl.BlockSpec((tm // KC, r, KC), lambda t: (t, 0, 0))
    in_specs = [row(D_MODEL), row(1), _const_spec(invl.shape), _const_spec(gmix.shape),
                _const_spec(win.shape), _const_spec(wvat.shape), _const_spec(wwit.shape),
                _const_spec(gcq.shape), _const_spec(wuq.shape), _const_spec(gckv.shape),
                _const_spec(wukvk.shape), _const_spec(wukvvt.shape)]
    out_shape = [
        jax.ShapeDtypeStruct((n_tok, C_QI - C_QA), bf),
        jax.ShapeDtypeStruct((n_tok, C_KA - C_QI), bf),
        jax.ShapeDtypeStruct((n_tok, N_HEADS_B * HEAD_SLOT), bf),
        jax.ShapeDtypeStruct((SUBLANE, n_tok), jnp.float32),
        jax.ShapeDtypeStruct((n_tok, KA_PAD), bf),
        jax.ShapeDtypeStruct((n_tok, KI_PAD), bf),
        jax.ShapeDtypeStruct((n_tok, N_HEADS_B * HEAD_SLOT), bf),
        jax.ShapeDtypeStruct((n_tok // KC, N_KV_HEADS_A * V_ROWS, KC), bf),
        jax.ShapeDtypeStruct((n_tok // KC, N_HEADS_B * V_ROWS, KC), bf),
        jax.ShapeDtypeStruct((n_tok, D_MODEL), bf),
        jax.ShapeDtypeStruct((n_tok, D_MODEL), bf),
        jax.ShapeDtypeStruct((n_tok, LANE), jnp.float32),
    ]
    out_specs = [row(C_QI - C_QA), row(C_KA - C_QI), row(N_HEADS_B * HEAD_SLOT), col(SUBLANE),
                 row(KA_PAD), row(KI_PAD), row(N_HEADS_B * HEAD_SLOT),
                 chunks(N_KV_HEADS_A * V_ROWS), chunks(N_HEADS_B * V_ROWS),
                 row(D_MODEL), row(D_MODEL), row(LANE)]
    return pl.pallas_call(
        _proj_kernel,
        grid=(n_tok // tm,),
        in_specs=in_specs,
        out_specs=out_specs,
        out_shape=out_shape,
        compiler_params=pltpu.CompilerParams(
            dimension_semantics=("arbitrary",), vmem_limit_bytes=VMEM_LIMIT),
        name="proj",
    )(x2, posc, invl, gmix, win, wvat, wwit, gcq, wuq, gckv, wukvk, wukvvt)


def _col_sum(a):
    r = a.shape[0]
    part = jnp.sum(a.reshape(r // SUBLANE, SUBLANE, a.shape[1]), axis=0)
    return jnp.sum(part, axis=0, keepdims=True)


def _col_max(a):
    r = a.shape[0]
    part = jnp.max(a.reshape(r // SUBLANE, SUBLANE, a.shape[1]), axis=0)
    return jnp.max(part, axis=0, keepdims=True)


def _attn_kernel(qa_ref, qi_ref, qb_ref, wit_ref, posq_ref,
                 kap_ref, kip_ref, kb_ref, vat_ref, vbt_ref, posk_ref,
                 yat_ref, ybt_ref,
                 score_ref, sb_ref, lg_ref, acc_ref, m_ref, mu_ref, *, topk):
    bf = jnp.bfloat16
    f32 = jnp.float32
    i = pl.program_id(1)
    nch = i + 1
    row_i = lax.broadcasted_iota(jnp.int32, (KC, TQ), 0)
    col_i = lax.broadcasted_iota(jnp.int32, (KC, TQ), 1)
    causal = row_i <= col_i
    kf = float(topk)

    def kslice(c):
        if isinstance(c, int):
            return pl.ds(c * KC, KC)
        return pl.ds(pl.multiple_of(c * KC, KC), KC)

    w = wit_ref[...]

    def score_chunk(c, carry):
        acc = jnp.zeros((KC, TQ), f32)
        for h in range(N_IDX_HEADS):
            kk = kip_ref[kslice(c), (h % 2) * LANE:(h % 2 + 1) * LANE]
            qq = qi_ref[:, (h // 2) * LANE:(h // 2 + 1) * LANE]
            d = lax.dot_general(kk, qq, _NT, preferred_element_type=f32)
            acc = acc + jnp.maximum(d, 0.0) * w[h:h + 1, :]
        score_ref[c] = acc
        sb_ref[c] = acc.astype(bf)
        return carry

    lax.fori_loop(0, nch, score_chunk, 0)
    diag = jnp.where(causal, score_ref[i], -jnp.inf)
    score_ref[i] = diag
    sb_ref[i] = diag.astype(bf)

    def count_rounded(thr_b):
        def body(c, acc):
            hit = (sb_ref[c] >= thr_b).reshape(KC // COUNT_ROWS, COUNT_ROWS, TQ)
            for r in range(KC // COUNT_ROWS):
                acc = jnp.where(hit[r], acc + 1.0, acc)
            return acc
        acc = lax.fori_loop(0, nch, body, jnp.zeros((COUNT_ROWS, TQ), bf))
        return _col_sum(acc.astype(f32))

    def key16_to_bf16(key):
        bits = jnp.where(key >= 0, key, key ^ jnp.int32(0x7FFF))
        return lax.bitcast_convert_type(lax.shift_left(bits, jnp.int32(16)), f32).astype(bf)

    def coarse_step(j, key):
        cand = key + lax.shift_left(jnp.int32(1), jnp.int32(15) - j)
        cnt = count_rounded(key16_to_bf16(cand))
        return jnp.where(cnt >= kf, cand, key)

    key16 = lax.fori_loop(0, 16, coarse_step, jnp.full((1, TQ), -(1 << 15), jnp.int32))
    fine_lo = lax.shift_left(key16 - jnp.where(key16 >= 0, 1, 0), jnp.int32(16))

    def count(pred):
        def body(c, acc):
            hit = pred(score_ref[c]).reshape(KC // COUNT_ROWS, COUNT_ROWS, TQ)
            for r in range(KC // COUNT_ROWS):
                acc = jnp.where(hit[r], acc + 1.0, acc)
            return acc
        acc = lax.fori_loop(0, nch, body, jnp.zeros((COUNT_ROWS, TQ), f32))
        return _col_sum(acc)

    def key_to_float(key):
        bits = jnp.where(key >= 0, key, key ^ jnp.int32(0x7FFFFFFF))
        return lax.bitcast_convert_type(bits, f32)

    def fine_step(j, off):
        cand = off + lax.shift_left(jnp.int32(1), jnp.int32(FINE_BITS - 1) - j)
        thr_c = key_to_float(fine_lo + cand)
        cnt = count(lambda s: s >= thr_c)
        return jnp.where(cnt >= kf, cand, off)

    key = fine_lo + lax.fori_loop(0, FINE_BITS, fine_step, jnp.zeros((1, TQ), jnp.int32))
    t_glob = i * TQ + lax.broadcasted_iota(jnp.int32, (1, TQ), 1)
    take_all = t_glob < topk
    thr = jnp.where(take_all, jnp.finfo(f32).min, key_to_float(key))
    n_ge = count(lambda s: s >= thr)
    surplus = jnp.where(take_all, 0.0, n_ge - kf)
    posq = posq_ref[...]

    any_surplus = jnp.max(surplus) > 0.0

    def distance(c):
        pk = posk_ref[kslice(c), :]
        return jnp.abs(jnp.concatenate([pk] * (TQ // LANE), axis=1) - posq)

    @pl.when(jnp.logical_not(any_surplus))
    def _select_all_ties():
        def chunk(c, carry):
            score_ref[c] = jnp.where(score_ref[c] >= thr, distance(c), FAR)
            return carry

        lax.fori_loop(0, nch, chunk, 0)

    @pl.when(any_surplus)
    def _select_ranked_ties():
        n_gt = count(lambda s: s > thr)
        need = jnp.where(take_all, float(score_ref.shape[0] * KC + 1), kf - n_gt)
        tri = jnp.where(lax.broadcasted_iota(jnp.int32, (KC, KC), 0)
                        >= lax.broadcasted_iota(jnp.int32, (KC, KC), 1), 1.0, 0.0).astype(bf)

        def chunk(c, taken):
            s = score_ref[c]
            eq = s == thr
            pc = jnp.dot(tri, jnp.where(eq, 1.0, 0.0).astype(bf),
                         preferred_element_type=f32) + taken
            rank = jnp.where(eq, pc, jnp.where(s > thr, -jnp.inf, jnp.inf))
            score_ref[c] = jnp.where(rank <= need, distance(c), FAR)
            return pc[KC - 1:KC, :]

        lax.fori_loop(0, nch, chunk, jnp.zeros((1, TQ), f32))

    def dsa_logits(c, h, first):
        g = h // GROUP_A
        kk = kap_ref[kslice(c), (2 * g + h % 2) * LANE:(2 * g + h % 2 + 1) * LANE]
        qq = qa_ref[:, (h // 2) * LANE:(h // 2 + 1) * LANE]
        slope = 2.0 ** (-8.0 * (h + 1) / N_HEADS_A) * LOG2E
        qk = lax.dot_general(kk, qq, _NT, preferred_element_type=f32)
        return qk - slope * score_ref[c]

    def dsa_values(c, h):
        g = h // GROUP_A
        return vat_ref[c, g * V_ROWS:(g + 1) * V_ROWS, :]

    def mla_chunk_of(j):
        return jnp.where(j == 0, i, j - 1)

    def mla_logits(c, h, first):
        kk = kb_ref[kslice(c), h * HEAD_SLOT:(h + 1) * HEAD_SLOT]
        qq = qb_ref[:, h * HEAD_SLOT:(h + 1) * HEAD_SLOT]
        qk = lax.dot_general(kk, qq, _NT, preferred_element_type=f32)
        return jnp.where(causal, qk, NEG) if first else qk

    def mla_values(c, h):
        return vbt_ref[c, h * V_ROWS:(h + 1) * V_ROWS, :]

    n_slots = N_HEADS_A + N_HEADS_B
    m_ref[...] = jnp.full(m_ref.shape, NEG, f32)
    mu_ref[...] = jnp.full(mu_ref.shape, NEG, f32)
    acc_ref[...] = jnp.zeros(acc_ref.shape, f32)

    def slot_logits(j, s, first):
        if s < N_HEADS_A:
            return dsa_logits(j, s, first)
        return mla_logits(mla_chunk_of(j), s - N_HEADS_A, first)

    def slot_values(j, s):
        if s < N_HEADS_A:
            return dsa_values(j, s)
        return mla_values(mla_chunk_of(j), s - N_HEADS_A)

    def stage(j, s, first):
        lg = slot_logits(j, s, first)
        lg_ref[s] = lg
        m_ref[s:s + 1, :] = jnp.maximum(m_ref[s:s + 1, :], _col_max(lg))

    def consume(j, s):
        m_now = m_ref[s:s + 1, :]
        alpha = jnp.exp2(mu_ref[s:s + 1, :] - m_now)
        mu_ref[s:s + 1, :] = m_now
        p = jnp.exp2(lg_ref[s] - m_now)
        pv = jnp.dot(slot_values(j, s), p.astype(bf), preferred_element_type=f32)
        acc_ref[s] = alpha * acc_ref[s] + pv

    slot_order = [s for pair in zip(range(N_HEADS_A), range(N_HEADS_A, n_slots)) for s in pair]
    for s in slot_order:
        stage(0, s, True)

    def sweep(j, carry):
        for s in slot_order:
            consume(j, s)
            stage(j + 1, s, False)
        return carry

    def sweep_pair(jj, carry):
        sweep(2 * jj, carry)
        return sweep(2 * jj + 1, carry)

    lax.fori_loop(0, i // 2, sweep_pair, 0)

    @pl.when(i % 2 == 1)
    def _odd_step():
        sweep(i - 1, 0)
    for s in range(n_slots):
        consume(i, s)
        out_ref, h, dv = (yat_ref, s, HEAD_DIM_A) if s < N_HEADS_A else (ybt_ref, s - N_HEADS_A, V_DIM_B)
        inv_l = 1.0 / acc_ref[s, dv:dv + 1, :]
        out_ref[h * dv:(h + 1) * dv, :] = (acc_ref[s, :dv, :] * inv_l).astype(out_ref.dtype)


def _attn_call(qa, qi, qb, wit, posr, kap, kip, kb, vat, vbt, posc, batch, seq, topk):
    nq = seq // TQ
    bf = jnp.bfloat16
    n_tok = batch * seq
    n_slots = N_HEADS_A + N_HEADS_B
    assert HEAD_DIM_A == V_DIM_B
    qrow = lambda w: pl.BlockSpec((TQ, w), lambda b, i: (b * nq + i, 0))
    qcol = lambda r: pl.BlockSpec((r, TQ), lambda b, i: (0, b * nq + i))
    krow = lambda w: pl.BlockSpec((seq, w), lambda b, i: (b, 0))
    kchunks = lambda r: pl.BlockSpec((seq // KC, r, KC), lambda b, i: (b, 0, 0))
    in_specs = [qrow(qa.shape[1]), qrow(qi.shape[1]), qrow(qb.shape[1]), qcol(SUBLANE), qcol(1),
                krow(kap.shape[1]), krow(kip.shape[1]), krow(kb.shape[1]),
                kchunks(vat.shape[1]), kchunks(vbt.shape[1]), krow(LANE)]
    out_shape = [jax.ShapeDtypeStruct((N_HEADS_A * HEAD_DIM_A, n_tok), bf),
                 jax.ShapeDtypeStruct((N_HEADS_B * V_DIM_B, n_tok), bf)]
    out_specs = [qcol(N_HEADS_A * HEAD_DIM_A), qcol(N_HEADS_B * V_DIM_B)]
    scratch = [pltpu.VMEM((seq // KC, KC, TQ), jnp.float32),
               pltpu.VMEM((seq // KC, KC, TQ), bf),
               pltpu.VMEM((n_slots, KC, TQ), jnp.float32),
               pltpu.VMEM((n_slots, V_ROWS, TQ), jnp.float32),
               pltpu.VMEM((n_slots, TQ), jnp.float32),
               pltpu.VMEM((n_slots, TQ), jnp.float32)]
    return pl.pallas_call(
        functools.partial(_attn_kernel, topk=topk),
        grid=(batch, nq),
        in_specs=in_specs,
        out_specs=out_specs,
        out_shape=out_shape,
        scratch_shapes=scratch,
        compiler_params=pltpu.CompilerParams(
            dimension_semantics=("arbitrary", "arbitrary"), vmem_limit_bytes=VMEM_LIMIT),
        name="attn",
    )(qa, qi, qb, wit, posr, kap, kip, kb, vat, vbt, posc)


def _post_kernel(x_ref, yat_ref, ybt_ref, sga_ref, sgb_ref, p_ref,
                 wpa_ref, wpb_ref, wout_ref, gmlp_ref, wff1_ref, wff2_ref, gple_ref,
                 wpg_ref, wple_ref, gfin_ref, o_ref, *, final_norm):
    bf = jnp.bfloat16
    f32 = jnp.float32
    a = lax.dot_general(yat_ref[...], wpa_ref[...], _TN, preferred_element_type=f32)
    b = lax.dot_general(ybt_ref[...], wpb_ref[...], _TN, preferred_element_type=f32)
    merged = sga_ref[...].astype(f32) * a + sgb_ref[...].astype(f32) * b
    h = x_ref[...] + jnp.dot(merged.astype(bf), wout_ref[...], preferred_element_type=f32)
    n2 = _rms(h, gmlp_ref[...]).astype(bf)
    ff = jnp.zeros_like(h)
    for c in range(D_FF // FF_CHUNK):
        sl = slice(c * FF_CHUNK, (c + 1) * FF_CHUNK)
        u = jnp.maximum(jnp.dot(n2, wff1_ref[:, sl], preferred_element_type=f32), 0.0)
        ff = ff + jnp.dot((u * u).astype(bf), wff2_ref[sl, :], preferred_element_type=f32)
    h = h + ff
    n3 = _rms(h, gple_ref[...]).astype(bf)
    gate = jax.nn.sigmoid(jnp.dot(n3, wpg_ref[...], preferred_element_type=f32))
    pe = jnp.dot(p_ref[...].astype(bf), wple_ref[...], preferred_element_type=f32)
    h = h + gate * pe
    o_ref[...] = _rms(h, gfin_ref[...]) if final_norm else h


def _post_call(x2, yat, ybt, sga, sgb, p2, wpa, wpb, wout, gmlp, wff1, wff2, gple, wpg, wple, gfin,
               final_norm):
    n_tok = x2.shape[0]
    tm = TM_POST
    row = lambda w: pl.BlockSpec((tm, w), lambda t: (t, 0))
    col = lambda r: pl.BlockSpec((r, tm), lambda t: (0, t))
    wspec = lambda a: pl.BlockSpec(a.shape, lambda t: (0,) * a.ndim, pipeline_mode=pl.Buffered(1))
    in_specs = [row(D_MODEL), col(yat.shape[0]), col(ybt.shape[0]), row(D_MODEL), row(D_MODEL),
                row(PLE_DIM)] + [wspec(a) for a in
                                 (wpa, wpb, wout, gmlp, wff1, wff2, gple, wpg, wple, gfin)]
    return pl.pallas_call(
        functools.partial(_post_kernel, final_norm=final_norm),
        grid=(n_tok // tm,),
        in_specs=in_specs,
        out_specs=row(D_MODEL),
        out_shape=jax.ShapeDtypeStruct((n_tok, D_MODEL), jnp.float32),
        compiler_params=pltpu.CompilerParams(
            dimension_semantics=("arbitrary",), vmem_limit_bytes=VMEM_LIMIT),
        name="post",
    )(x2, yat, ybt, sga, sgb, p2, wpa, wpb, wout, gmlp, wff1, wff2, gple, wpg, wple, gfin)


def _pack_layer(w_in, w_uq, w_ukv):
    bf = jnp.bfloat16
    sizes = (N_HEADS_A * HEAD_DIM_A, N_KV_HEADS_A * HEAD_DIM_A, N_KV_HEADS_A * HEAD_DIM_A,
             N_IDX_HEADS * IDX_DIM, IDX_DIM, N_IDX_HEADS, Q_LORA, KV_LORA, ROPE_DIM,
             D_MODEL, D_MODEL)
    offs = np.concatenate([[0], np.cumsum(sizes)])
    wqa, wka, wva, wqi, wki, wwi, wcq, wckv, wkr, wga, wgb = [
        w_in[:, offs[j]:offs[j + 1]] for j in range(len(sizes))]
    d = w_in.shape[0]
    small_pad = jnp.zeros((d, LANE - IDX_DIM - ROPE_DIM), w_in.dtype)
    win = jnp.concatenate([wqa, wqi, wka, wki, wkr, small_pad, wcq, wckv, wga, wgb],
                          axis=1).astype(bf)
    wvat = wva.T.astype(bf)
    wwit = jnp.concatenate([wwi.T, jnp.zeros((SUBLANE - N_IDX_HEADS, d), w_in.dtype)],
                           axis=0).astype(bf)
    uq = w_uq.reshape(Q_LORA, N_HEADS_B, NOPE_DIM + ROPE_DIM)
    uq = jnp.pad(uq, ((0, 0), (0, 0), (0, HEAD_SLOT - NOPE_DIM - ROPE_DIM)))
    wuq = uq.reshape(Q_LORA, N_HEADS_B * HEAD_SLOT).astype(bf)
    ukv = w_ukv.reshape(KV_LORA, N_HEADS_B, NOPE_DIM + V_DIM_B)
    uk = jnp.pad(ukv[:, :, :NOPE_DIM], ((0, 0), (0, 0), (0, HEAD_SLOT - NOPE_DIM)))
    wukvk = uk.reshape(KV_LORA, N_HEADS_B * HEAD_SLOT).astype(bf)
    wukvvt = ukv[:, :, NOPE_DIM:].reshape(KV_LORA, N_HEADS_B * V_DIM_B).T.astype(bf)
    return win, wvat, wwit, wuq, wukvk, wukvvt


def kernel(x, p, positions, g_mix, w_in, g_cq, w_uq, g_ckv, w_ukv, w_pa, w_pb, w_out, g_mlp,
           w_ff1, w_ff2, g_ple, w_ple_gate, w_ple, g_final):
    bf = jnp.bfloat16
    batch, seq, d = x.shape
    depth = w_in.shape[0]
    n_tok = batch * seq
    topk = min(TOPK_MAX, seq // 4)
    assert d == D_MODEL and seq % TQ == 0 and TQ == KC and topk <= TQ and n_tok % TM_PROJ == 0
    assert IDX_DIM == NOPE_DIM == HEAD_DIM_A == LANE // 2

    posf = positions.astype(jnp.float32)
    posc = posf.reshape(n_tok, 1)
    posr = posf.reshape(1, n_tok)
    inv = ROPE_THETA ** (-jnp.arange(HALF_ROPE, dtype=jnp.float32) / HALF_ROPE)
    invl = jnp.concatenate([jnp.zeros((NOPE_DIM,), jnp.float32), inv, inv,
                            jnp.zeros((HEAD_SLOT - NOPE_DIM - ROPE_DIM,), jnp.float32)]).reshape(1, LANE)

    h = x.reshape(n_tok, d)
    for li in range(depth):
        win, wvat, wwit, wuq, wukvk, wukvvt = _pack_layer(w_in[li], w_uq[li], w_ukv[li])
        (qa, qi, qb, wit, kap, kip, kb, vat, vbt, sga, sgb, posb) = _proj_call(
            h, posc, invl, g_mix[li].reshape(1, d), win, wvat, wwit, g_cq[li].reshape(1, -1),
            wuq, g_ckv[li].reshape(1, -1), wukvk, wukvvt)
        yat, ybt = _attn_call(qa, qi, qb, wit, posr, kap, kip, kb, vat, vbt, posb, batch, seq, topk)
        h = _post_call(h, yat, ybt, sga, sgb, p[li].reshape(n_tok, -1),
                       w_pa[li].astype(bf), w_pb[li].astype(bf), w_out[li].astype(bf),
                       g_mlp[li].reshape(1, d), w_ff1[li].astype(bf), w_ff2[li].astype(bf),
                       g_ple[li].reshape(1, d), w_ple_gate[li].astype(bf), w_ple[li].astype(bf),
                       g_final.reshape(1, d), li == depth - 1)
    return h.reshape(batch, seq, d)
```

```python
import functools

import jax
import jax.numpy as jnp
import numpy as np
from jax import lax
from jax.experimental import pallas as pl
from jax.experimental.pallas import tpu as pltpu

D_MODEL = 1024
N_HEADS_A = 8
N_KV_HEADS_A = 2
HEAD_DIM_A = 64
GROUP_A = N_HEADS_A // N_KV_HEADS_A
N_IDX_HEADS = 4
IDX_DIM = 64
TOPK_MAX = 256
N_HEADS_B = 8
Q_LORA = 384
KV_LORA = 256
NOPE_DIM = 64
ROPE_DIM = 32
HALF_ROPE = ROPE_DIM // 2
V_DIM_B = 64
ROPE_THETA = 10000.0
D_FF = 4 * D_MODEL
PLE_DIM = 256
EPS = 1e-6

LANE = 128
SUBLANE = 8
HEAD_SLOT = 128
V_ROWS = V_DIM_B + 2 * SUBLANE
VMEM_LIMIT = 56 * 1024 * 1024

TM_PROJ = 1024
TM_POST = 512
TQ = 256
KC = 256
FF_CHUNK = 1024
NEG = -1e30
FAR = 1e30
LOG2E = 1.4426950408889634
COUNT_ROWS = 32
KEY_HALVES = 2
FINE_BITS = 17

C_QA = 0
C_QI = C_QA + N_HEADS_A * HEAD_DIM_A
C_KA = C_QI + N_IDX_HEADS * IDX_DIM
C_SM = C_KA + N_KV_HEADS_A * HEAD_DIM_A
C_CQ = C_SM + LANE
C_CKV = C_CQ + Q_LORA
C_GA = C_CKV + KV_LORA
C_GB = C_GA + D_MODEL
C_END = C_GB + D_MODEL
KA_PAD = 2 * N_KV_HEADS_A * LANE
KI_PAD = 2 * LANE

_NT = (((1,), (1,)), ((), ()))
_TN = (((0,), (0,)), ((), ()))


def _rms(x, g):
    return x * lax.rsqrt(jnp.mean(x * x, axis=-1, keepdims=True) + EPS) * g


def _rope_slot(x, cos_t, sin_dn, sin_up):
    up = pltpu.roll(x, HALF_ROPE, 1)
    dn = pltpu.roll(x, LANE - HALF_ROPE, 1)
    return x * cos_t + dn * sin_dn + up * sin_up


def _proj_kernel(x_ref, pos_ref, invl_ref, gmix_ref, win_ref, wvat_ref, wwit_ref, gcq_ref,
                 wuq_ref, gckv_ref, wukvk_ref, wukvvt_ref,
                 qa_ref, qi_ref, qb_ref, wit_ref, kap_ref, kip_ref, kb_ref, vat_ref, vbt_ref,
                 sga_ref, sgb_ref, posb_ref):
    bf = jnp.bfloat16
    n = _rms(x_ref[...], gmix_ref[...]).astype(bf)
    posb_ref[...] = jnp.broadcast_to(pos_ref[...], posb_ref.shape)

    def cols(c0, c1):
        return jnp.dot(n, win_ref[:, c0:c1], preferred_element_type=jnp.float32)

    cqn = _rms(cols(C_CQ, C_CKV), gcq_ref[...]).astype(bf)
    ckvn = _rms(cols(C_CKV, C_GA), gckv_ref[...]).astype(bf)

    lane = lax.broadcasted_iota(jnp.int32, (1, LANE), 1)
    is_x1 = (lane >= NOPE_DIM) & (lane < NOPE_DIM + HALF_ROPE)
    is_x2 = (lane >= NOPE_DIM + HALF_ROPE) & (lane < NOPE_DIM + ROPE_DIM)
    ang = pos_ref[...] * invl_ref[...]
    cos, sin = jnp.cos(ang), jnp.sin(ang)
    cos_t = jnp.where(is_x1 | is_x2, cos, 1.0)
    sin_dn = jnp.where(is_x1, -sin, 0.0)
    sin_up = jnp.where(is_x2, sin, 0.0)

    qa_ref[...] = (cols(C_QA, C_QI) * (HEAD_DIM_A ** -0.5 * LOG2E)).astype(bf)
    qi_ref[...] = cols(C_QI, C_KA).astype(bf)
    low = lane < LANE // 2

    def half_slabs(z):
        swapped = pltpu.roll(z, LANE // 2, 1)
        return (jnp.where(low, z, 0.0), jnp.where(low, 0.0, swapped),
                jnp.where(low, swapped, 0.0), jnp.where(low, 0.0, z))

    for j, slab in enumerate(half_slabs(cols(C_KA, C_SM))):
        kap_ref[:, j * LANE:(j + 1) * LANE] = slab.astype(bf)
    small = cols(C_SM, C_CQ)
    ki_slabs = half_slabs(small)
    kip_ref[:, :LANE] = ki_slabs[0].astype(bf)
    kip_ref[:, LANE:] = ki_slabs[1].astype(bf)
    sga_ref[...] = jax.nn.sigmoid(cols(C_GA, C_GB)).astype(bf)
    sgb_ref[...] = jax.nn.sigmoid(cols(C_GB, C_END)).astype(bf)
    def put_chunks(ref, vt):
        ones = jnp.ones((V_ROWS - V_DIM_B, KC), bf)
        for j in range(TM_PROJ // KC):
            for g in range(vt.shape[0] // V_DIM_B):
                ref[j, g * V_ROWS:g * V_ROWS + V_DIM_B, :] = (
                    vt[g * V_DIM_B:(g + 1) * V_DIM_B, j * KC:(j + 1) * KC].astype(bf))
                ref[j, g * V_ROWS + V_DIM_B:(g + 1) * V_ROWS, :] = ones

    put_chunks(vat_ref, lax.dot_general(wvat_ref[...], n, _NT, preferred_element_type=jnp.float32))
    wit_ref[...] = lax.dot_general(wwit_ref[...], n, _NT,
                                   preferred_element_type=jnp.float32) * ((N_IDX_HEADS * IDX_DIM) ** -0.5)

    qscale = (NOPE_DIM + ROPE_DIM) ** -0.5 * LOG2E
    qall = jnp.dot(cqn, wuq_ref[...], preferred_element_type=jnp.float32)
    for h in range(N_HEADS_B):
        sl = slice(h * HEAD_SLOT, (h + 1) * HEAD_SLOT)
        qb_ref[:, sl] = (_rope_slot(qall[:, sl], cos_t, sin_dn, sin_up) * qscale).astype(bf)

    kr = _rope_slot(jnp.where(low, 0.0, small), cos_t, sin_dn, sin_up)
    kall = jnp.dot(ckvn, wukvk_ref[...], preferred_element_type=jnp.float32)
    for h in range(N_HEADS_B):
        sl = slice(h * HEAD_SLOT, (h + 1) * HEAD_SLOT)
        kb_ref[:, sl] = (kall[:, sl] + kr).astype(bf)
    put_chunks(vbt_ref, lax.dot_general(wukvvt_ref[...], ckvn, _NT,
                                        preferred_element_type=jnp.float32))


def _const_spec(shape):
    return pl.BlockSpec(shape, lambda *_: (0,) * len(shape), pipeline_mode=pl.Buffered(1))


def _proj_call(x2, posc, invl, gmix, win, wvat, wwit, gcq, wuq, gckv, wukvk, wukvvt):
    n_tok = x2.shape[0]
    tm = TM_PROJ
    bf = jnp.bfloat16
    row = lambda w: pl.BlockSpec((tm, w), lambda t: (t, 0))
    col = lambda r: pl.BlockSpec((r, tm), lambda t: (0, t))
    chunks = lambda r: pl.BlockSpec((tm // KC, r, KC), lambda t: (t, 0, 0))
    in_specs = [row(D_MODEL), row(1), _const_spec(invl.shape), _const_spec(gmix.shape),
                _const_spec(win.shape), _const_spec(wvat.shape), _const_spec(wwit.shape),
                _const_spec(gcq.shape), _const_spec(wuq.shape), _const_spec(gckv.shape),
                _const_spec(wukvk.shape), _const_spec(wukvvt.shape)]
    out_shape = [
        jax.ShapeDtypeStruct((n_tok, C_QI - C_QA), bf),
        jax.ShapeDtypeStruct((n_tok, C_KA - C_QI), bf),
        jax.ShapeDtypeStruct((n_tok, N_HEADS_B * HEAD_SLOT), bf),
        jax.ShapeDtypeStruct((SUBLANE, n_tok), jnp.float32),
        jax.ShapeDtypeStruct((n_tok, KA_PAD), bf),
        jax.ShapeDtypeStruct((n_tok, KI_PAD), bf),
        jax.ShapeDtypeStruct((n_tok, N_HEADS_B * HEAD_SLOT), bf),
        jax.ShapeDtypeStruct((n_tok // KC, N_KV_HEADS_A * V_ROWS, KC), bf),
        jax.ShapeDtypeStruct((n_tok // KC, N_HEADS_B * V_ROWS, KC), bf),
        jax.ShapeDtypeStruct((n_tok, D_MODEL), bf),
        jax.ShapeDtypeStruct((n_tok, D_MODEL), bf),
        jax.ShapeDtypeStruct((n_tok, LANE), jnp.float32),
    ]
    out_specs = [row(C_QI - C_QA), row(C_KA - C_QI), row(N_HEADS_B * HEAD_SLOT), col(SUBLANE),
                 row(KA_PAD), row(KI_PAD), row(N_HEADS_B * HEAD_SLOT),
                 chunks(N_KV_HEADS_A * V_ROWS), chunks(N_HEADS_B * V_ROWS),
                 row(D_MODEL), row(D_MODEL), row(LANE)]
    return pl.pallas_call(
        _proj_kernel,
        grid=(n_tok // tm,),
        in_specs=in_specs,
        out_specs=out_specs,
        out_shape=out_shape,
        compiler_params=pltpu.CompilerParams(
            dimension_semantics=("arbitrary",), vmem_limit_bytes=VMEM_LIMIT),
        name="proj",
    )(x2, posc, invl, gmix, win, wvat, wwit, gcq, wuq, gckv, wukvk, wukvvt)


def _col_sum(a):
    r = a.shape[0]
    part = jnp.sum(a.reshape(r // SUBLANE, SUBLANE, a.shape[1]), axis=0)
    return jnp.sum(part, axis=0, keepdims=True)


def _col_max(a):
    r = a.shape[0]
    part = jnp.max(a.reshape(r // SUBLANE, SUBLANE, a.shape[1]), axis=0)
    return jnp.max(part, axis=0, keepdims=True)


def _attn_kernel(qa_ref, qi_ref, qb_ref, wit_ref, posq_ref,
                 kap_ref, kip_ref, kb_ref, vat_ref, vbt_ref, posk_ref,
                 yat_ref, ybt_ref,
                 score_ref, sb_ref, lg_ref, acc_ref, m_ref, mu_ref, *, topk):
    bf = jnp.bfloat16
    f32 = jnp.float32
    i = pl.program_id(1)
    nch = i + 1
    row_i = lax.broadcasted_iota(jnp.int32, (KC, TQ), 0)
    col_i = lax.broadcasted_iota(jnp.int32, (KC, TQ), 1)
    causal = row_i <= col_i
    kf = float(topk)

    def kslice(c):
        if isinstance(c, int):
            return pl.ds(c * KC, KC)
        return pl.ds(pl.multiple_of(c * KC, KC), KC)

    w = wit_ref[...]

    def score_chunk(c, carry):
        acc = jnp.zeros((KC, TQ), f32)
        for h in range(N_IDX_HEADS):
            kk = kip_ref[kslice(c), (h % 2) * LANE:(h % 2 + 1) * LANE]
            qq = qi_ref[:, (h // 2) * LANE:(h // 2 + 1) * LANE]
            d = lax.dot_general(kk, qq, _NT, preferred_element_type=f32)
            acc = acc + jnp.maximum(d, 0.0) * w[h:h + 1, :]
        score_ref[c] = acc
        sb_ref[c] = acc.astype(bf)
        return carry

    lax.fori_loop(0, nch, score_chunk, 0)
    diag = jnp.where(causal, score_ref[i], -jnp.inf)
    score_ref[i] = diag
    sb_ref[i] = diag.astype(bf)

    def count_rounded(thr_b):
        def body(c, acc):
            hit = (sb_ref[c] >= thr_b).reshape(KC // COUNT_ROWS, COUNT_ROWS, TQ)
            for r in range(KC // COUNT_ROWS):
                acc = jnp.where(hit[r], acc + 1.0, acc)
            return acc
        acc = lax.fori_loop(0, nch, body, jnp.zeros((COUNT_ROWS, TQ), bf))
        return _col_sum(acc.astype(f32))

    def key16_to_bf16(key):
        bits = jnp.where(key >= 0, key, key ^ jnp.int32(0x7FFF))
        return lax.bitcast_convert_type(lax.shift_left(bits, jnp.int32(16)), f32).astype(bf)

    def coarse_step(j, key):
        cand = key + lax.shift_left(jnp.int32(1), jnp.int32(15) - j)
        cnt = count_rounded(key16_to_bf16(cand))
        return jnp.where(cnt >= kf, cand, key)

    key16 = lax.fori_loop(0, 16, coarse_step, jnp.full((1, TQ), -(1 << 15), jnp.int32))
    fine_lo = lax.shift_left(key16 - jnp.where(key16 >= 0, 1, 0), jnp.int32(16))

    def count(pred):
        def body(c, acc):
            hit = pred(score_ref[c]).reshape(KC // COUNT_ROWS, COUNT_ROWS, TQ)
            for r in range(KC // COUNT_ROWS):
                acc = jnp.where(hit[r], acc + 1.0, acc)
            return acc
        acc = lax.fori_loop(0, nch, body, jnp.zeros((COUNT_ROWS, TQ), f32))
        return _col_sum(acc)

    def key_to_float(key):
        bits = jnp.where(key >= 0, key, key ^ jnp.int32(0x7FFFFFFF))
        return lax.bitcast_convert_type(bits, f32)

    def fine_step(j, off):
        cand = off + lax.shift_left(jnp.int32(1), jnp.int32(FINE_BITS - 1) - j)
        thr_c = key_to_float(fine_lo + cand)
        cnt = count(lambda s: s >= thr_c)
        return jnp.where(cnt >= kf, cand, off)

    key = fine_lo + lax.fori_loop(0, FINE_BITS, fine_step, jnp.zeros((1, TQ), jnp.int32))
    t_glob = i * TQ + lax.broadcasted_iota(jnp.int32, (1, TQ), 1)
    take_all = t_glob < topk
    thr = jnp.where(take_all, jnp.finfo(f32).min, key_to_float(key))
    n_ge = count(lambda s: s >= thr)
    surplus = jnp.where(take_all, 0.0, n_ge - kf)
    posq = posq_ref[...]

    any_surplus = jnp.max(surplus) > 0.0

    def distance(c):
        pk = posk_ref[kslice(c), :]
        return jnp.abs(jnp.concatenate([pk] * (TQ // LANE), axis=1) - posq)

    @pl.when(jnp.logical_not(any_surplus))
    def _select_all_ties():
        def chunk(c, carry):
            score_ref[c] = jnp.where(score_ref[c] >= thr, distance(c), FAR)
            return carry

        lax.fori_loop(0, nch, chunk, 0)

    @pl.when(any_surplus)
    def _select_ranked_ties():
        n_gt = count(lambda s: s > thr)
        need = jnp.where(take_all, float(score_ref.shape[0] * KC + 1), kf - n_gt)
        tri = jnp.where(lax.broadcasted_iota(jnp.int32, (KC, KC), 0)
                        >= lax.broadcasted_iota(jnp.int32, (KC, KC), 1), 1.0, 0.0).astype(bf)

        def chunk(c, taken):
            s = score_ref[c]
            eq = s == thr
            pc = jnp.dot(tri, jnp.where(eq, 1.0, 0.0).astype(bf),
                         preferred_element_type=f32) + taken
            rank = jnp.where(eq, pc, jnp.where(s > thr, -jnp.inf, jnp.inf))
            score_ref[c] = jnp.where(rank <= need, distance(c), FAR)
            return pc[KC - 1:KC, :]

        lax.fori_loop(0, nch, chunk, jnp.zeros((1, TQ), f32))

    def dsa_logits(c, h, first):
        g = h // GROUP_A
        kk = kap_ref[kslice(c), (2 * g + h % 2) * LANE:(2 * g + h % 2 + 1) * LANE]
        qq = qa_ref[:, (h // 2) * LANE:(h // 2 + 1) * LANE]
        slope = 2.0 ** (-8.0 * (h + 1) / N_HEADS_A) * LOG2E
        qk = lax.dot_general(kk, qq, _NT, preferred_element_type=f32)
        return qk - slope * score_ref[c]

    def dsa_values(c, h):
        g = h // GROUP_A
        return vat_ref[c, g * V_ROWS:(g + 1) * V_ROWS, :]

    def mla_chunk_of(j):
        return jnp.where(j == 0, i, j - 1)

    def mla_logits(c, h, first):
        kk = kb_ref[kslice(c), h * HEAD_SLOT:(h + 1) * HEAD_SLOT]
        qq = qb_ref[:, h * HEAD_SLOT:(h + 1) * HEAD_SLOT]
        qk = lax.dot_general(kk, qq, _NT, preferred_element_type=f32)
        return jnp.where(causal, qk, NEG) if first else qk

    def mla_values(c, h):
        return vbt_ref[c, h * V_ROWS:(h + 1) * V_ROWS, :]

    n_slots = N_HEADS_A + N_HEADS_B
    m_ref[...] = jnp.full(m_ref.shape, NEG, f32)
    mu_ref[...] = jnp.full(mu_ref.shape, NEG, f32)
    acc_ref[...] = jnp.zeros(acc_ref.shape, f32)

    def slot_logits(j, s, first):
        if s < N_HEADS_A:
            return dsa_logits(j, s, first)
        return mla_logits(mla_chunk_of(j), s - N_HEADS_A, first)

    def slot_values(j, s):
        if s < N_HEADS_A:
            return dsa_values(j, s)
        return mla_values(mla_chunk_of(j), s - N_HEADS_A)

    def stage(j, s, first):
        lg = slot_logits(j, s, first)
        lg_ref[s] = lg
        m_ref[s:s + 1, :] = jnp.maximum(m_ref[s:s + 1, :], _col_max(lg))

    def consume(j, s):
        m_now = m_ref[s:s + 1, :]
        alpha = jnp.exp2(mu_ref[s:s + 1, :] - m_now)
        mu_ref[s:s + 1, :] = m_now
        vals = slot_values(j, s)
        pv = alpha * acc_ref[s]
        for r in range(KEY_HALVES):
            rows = slice(r * (KC // KEY_HALVES), (r + 1) * (KC // KEY_HALVES))
            p = jnp.exp2(lg_ref[s, rows, :] - m_now)
            pv = pv + jnp.dot(vals[:, rows], p.astype(bf), preferred_element_type=f32)
        acc_ref[s] = pv

    slot_order = [s for pair in zip(range(N_HEADS_A), range(N_HEADS_A, n_slots)) for s in pair]
    for s in slot_order:
        stage(0, s, True)

    def sweep(j, carry):
        for s in slot_order:
            consume(j, s)
            stage(j + 1, s, False)
        return carry

    def sweep_pair(jj, carry):
        sweep(2 * jj, carry)
        return sweep(2 * jj + 1, carry)

    lax.fori_loop(0, i // 2, sweep_pair, 0)

    @pl.when(i % 2 == 1)
    def _odd_step():
        sweep(i - 1, 0)
    for s in range(n_slots):
        consume(i, s)
        out_ref, h, dv = (yat_ref, s, HEAD_DIM_A) if s < N_HEADS_A else (ybt_ref, s - N_HEADS_A, V_DIM_B)
        inv_l = 1.0 / acc_ref[s, dv:dv + 1, :]
        out_ref[h * dv:(h + 1) * dv, :] = (acc_ref[s, :dv, :] * inv_l).astype(out_ref.dtype)


def _attn_call(qa, qi, qb, wit, posr, kap, kip, kb, vat, vbt, posc, batch, seq, topk):
    nq = seq // TQ
    bf = jnp.bfloat16
    n_tok = batch * seq
    n_slots = N_HEADS_A + N_HEADS_B
    assert HEAD_DIM_A == V_DIM_B
    qrow = lambda w: pl.BlockSpec((TQ, w), lambda b, i: (b * nq + i, 0))
    qcol = lambda r: pl.BlockSpec((r, TQ), lambda b, i: (0, b * nq + i))
    krow = lambda w: pl.BlockSpec((seq, w), lambda b, i: (b, 0))
    kchunks = lambda r: pl.BlockSpec((seq // KC, r, KC), lambda b, i: (b, 0, 0))
    in_specs = [qrow(qa.shape[1]), qrow(qi.shape[1]), qrow(qb.shape[1]), qcol(SUBLANE), qcol(1),
                krow(kap.shape[1]), krow(kip.shape[1]), krow(kb.shape[1]),
                kchunks(vat.shape[1]), kchunks(vbt.shape[1]), krow(LANE)]
    out_shape = [jax.ShapeDtypeStruct((N_HEADS_A * HEAD_DIM_A, n_tok), bf),
                 jax.ShapeDtypeStruct((N_HEADS_B * V_DIM_B, n_tok), bf)]
    out_specs = [qcol(N_HEADS_A * HEAD_DIM_A), qcol(N_HEADS_B * V_DIM_B)]
    scratch = [pltpu.VMEM((seq // KC, KC, TQ), jnp.float32),
               pltpu.VMEM((seq // KC, KC, TQ), bf),
               pltpu.VMEM((n_slots, KC, TQ), jnp.float32),
               pltpu.VMEM((n_slots, V_ROWS, TQ), jnp.float32),
               pltpu.VMEM((n_slots, TQ), jnp.float32),
               pltpu.VMEM((n_slots, TQ), jnp.float32)]
    return pl.pallas_call(
        functools.partial(_attn_kernel, topk=topk),
        grid=(batch, nq),
        in_specs=in_specs,
        out_specs=out_specs,
        out_shape=out_shape,
        scratch_shapes=scratch,
        compiler_params=pltpu.CompilerParams(
            dimension_semantics=("arbitrary", "arbitrary"), vmem_limit_bytes=VMEM_LIMIT),
        name="attn",
    )(qa, qi, qb, wit, posr, kap, kip, kb, vat, vbt, posc)


def _post_kernel(x_ref, yat_ref, ybt_ref, sga_ref, sgb_ref, p_ref,
                 wpa_ref, wpb_ref, wout_ref, gmlp_ref, wff1_ref, wff2_ref, gple_ref,
                 wpg_ref, wple_ref, gfin_ref, o_ref, *, final_norm):
    bf = jnp.bfloat16
    f32 = jnp.float32
    a = lax.dot_general(yat_ref[...], wpa_ref[...], _TN, preferred_element_type=f32)
    b = lax.dot_general(ybt_ref[...], wpb_ref[...], _TN, preferred_element_type=f32)
    merged = sga_ref[...].astype(f32) * a + sgb_ref[...].astype(f32) * b
    h = x_ref[...] + jnp.dot(merged.astype(bf), wout_ref[...], preferred_element_type=f32)
    n2 = _rms(h, gmlp_ref[...]).astype(bf)
    ff = jnp.zeros_like(h)
    for c in range(D_FF // FF_CHUNK):
        sl = slice(c * FF_CHUNK, (c + 1) * FF_CHUNK)
        u = jnp.maximum(jnp.dot(n2, wff1_ref[:, sl], preferred_element_type=f32), 0.0)
        ff = ff + jnp.dot((u * u).astype(bf), wff2_ref[sl, :], preferred_element_type=f32)
    h = h + ff
    n3 = _rms(h, gple_ref[...]).astype(bf)
    gate = jax.nn.sigmoid(jnp.dot(n3, wpg_ref[...], preferred_element_type=f32))
    pe = jnp.dot(p_ref[...].astype(bf), wple_ref[...], preferred_element_type=f32)
    h = h + gate * pe
    o_ref[...] = _rms(h, gfin_ref[...]) if final_norm else h


def _post_call(x2, yat, ybt, sga, sgb, p2, wpa, wpb, wout, gmlp, wff1, wff2, gple, wpg, wple, gfin,
               final_norm):
    n_tok = x2.shape[0]
    tm = TM_POST
    row = lambda w: pl.BlockSpec((tm, w), lambda t: (t, 0))
    col = lambda r: pl.BlockSpec((r, tm), lambda t: (0, t))
    wspec = lambda a: pl.BlockSpec(a.shape, lambda t: (0,) * a.ndim, pipeline_mode=pl.Buffered(1))
    in_specs = [row(D_MODEL), col(yat.shape[0]), col(ybt.shape[0]), row(D_MODEL), row(D_MODEL),
                row(PLE_DIM)] + [wspec(a) for a in
                                 (wpa, wpb, wout, gmlp, wff1, wff2, gple, wpg, wple, gfin)]
    return pl.pallas_call(
        functools.partial(_post_kernel, final_norm=final_norm),
        grid=(n_tok // tm,),
        in_specs=in_specs,
        out_specs=row(D_MODEL),
        out_shape=jax.ShapeDtypeStruct((n_tok, D_MODEL), jnp.float32),
        compiler_params=pltpu.CompilerParams(
            dimension_semantics=("arbitrary",), vmem_limit_bytes=VMEM_LIMIT),
        name="post",
    )(x2, yat, ybt, sga, sgb, p2, wpa, wpb, wout, gmlp, wff1, wff2, gple, wpg, wple, gfin)


def _pack_layer(w_in, w_uq, w_ukv):
    bf = jnp.bfloat16
    sizes = (N_HEADS_A * HEAD_DIM_A, N_KV_HEADS_A * HEAD_DIM_A, N_KV_HEADS_A * HEAD_DIM_A,
             N_IDX_HEADS * IDX_DIM, IDX_DIM, N_IDX_HEADS, Q_LORA, KV_LORA, ROPE_DIM,
             D_MODEL, D_MODEL)
    offs = np.concatenate([[0], np.cumsum(sizes)])
    wqa, wka, wva, wqi, wki, wwi, wcq, wckv, wkr, wga, wgb = [
        w_in[:, offs[j]:offs[j + 1]] for j in range(len(sizes))]
    d = w_in.shape[0]
    small_pad = jnp.zeros((d, LANE - IDX_DIM - ROPE_DIM), w_in.dtype)
    win = jnp.concatenate([wqa, wqi, wka, wki, wkr, small_pad, wcq, wckv, wga, wgb],
                          axis=1).astype(bf)
    wvat = wva.T.astype(bf)
    wwit = jnp.concatenate([wwi.T, jnp.zeros((SUBLANE - N_IDX_HEADS, d), w_in.dtype)],
                           axis=0).astype(bf)
    uq = w_uq.reshape(Q_LORA, N_HEADS_B, NOPE_DIM + ROPE_DIM)
    uq = jnp.pad(uq, ((0, 0), (0, 0), (0, HEAD_SLOT - NOPE_DIM - ROPE_DIM)))
    wuq = uq.reshape(Q_LORA, N_HEADS_B * HEAD_SLOT).astype(bf)
    ukv = w_ukv.reshape(KV_LORA, N_HEADS_B, NOPE_DIM + V_DIM_B)
    uk = jnp.pad(ukv[:, :, :NOPE_DIM], ((0, 0), (0, 0), (0, HEAD_SLOT - NOPE_DIM)))
    wukvk = uk.reshape(KV_LORA, N_HEADS_B * HEAD_SLOT).astype(bf)
    wukvvt = ukv[:, :, NOPE_DIM:].reshape(KV_LORA, N_HEADS_B * V_DIM_B).T.astype(bf)
    return win, wvat, wwit, wuq, wukvk, wukvvt


def kernel(x, p, positions, g_mix, w_in, g_cq, w_uq, g_ckv, w_ukv, w_pa, w_pb, w_out, g_mlp,
           w_ff1, w_ff2, g_ple, w_ple_gate, w_ple, g_final):
    bf = jnp.bfloat16
    batch, seq, d = x.shape
    depth = w_in.shape[0]
    n_tok = batch * seq
    topk = min(TOPK_MAX, seq // 4)
    assert d == D_MODEL and seq % TQ == 0 and TQ == KC and topk <= TQ and n_tok % TM_PROJ == 0
    assert IDX_DIM == NOPE_DIM == HEAD_DIM_A == LANE // 2

    posf = positions.astype(jnp.float32)
    posc = posf.reshape(n_tok, 1)
    posr = posf.reshape(1, n_tok)
    inv = ROPE_THETA ** (-jnp.arange(HALF_ROPE, dtype=jnp.float32) / HALF_ROPE)
    invl = jnp.concatenate([jnp.zeros((NOPE_DIM,), jnp.float32), inv, inv,
                            jnp.zeros((HEAD_SLOT - NOPE_DIM - ROPE_DIM,), jnp.float32)]).reshape(1, LANE)

    h = x.reshape(n_tok, d)
    for li in range(depth):
        win, wvat, wwit, wuq, wukvk, wukvvt = _pack_layer(w_in[li], w_uq[li], w_ukv[li])
        (qa, qi, qb, wit, kap, kip, kb, vat, vbt, sga, sgb, posb) = _proj_call(
            h, posc, invl, g_mix[li].reshape(1, d), win, wvat, wwit, g_cq[li].reshape(1, -1),
            wuq, g_ckv[li].reshape(1, -1), wukvk, wukvvt)
        yat, ybt = _attn_call(qa, qi, qb, wit, posr, kap, kip, kb, vat, vbt, posb, batch, seq, topk)
        h = _post_call(h, yat, ybt, sga, sgb, p[li].reshape(n_tok, -1),
                       w_pa[li].astype(bf), w_pb[li].astype(bf), w_out[li].astype(bf),
                       g_mlp[li].reshape(1, d), w_ff1[li].astype(bf), w_ff2[li].astype(bf),
                       g_ple[li].reshape(1, d), w_ple_gate[li].astype(bf), w_ple[li].astype(bf),
                       g_final.reshape(1, d), li == depth - 1)
    return h.reshape(batch, seq, d)
```
